```python
import jax, jax.numpy as jnp
from jax import lax
import numpy as np

D_MODEL = 1024
BATCH = 16
SEQ = 4096
DEPTH = 1

ATTN_GROUPS = ((128, 1), (512, 4), (2048, 16))
N_ATTN_GROUPS = len(ATTN_GROUPS)
ATTN_HEADS_PER_GROUP = 4
ATTN_HEADS = N_ATTN_GROUPS * ATTN_HEADS_PER_GROUP
ATTN_HEAD_DIM = 128
ATTN_OUT_WIDTH = ATTN_HEADS_PER_GROUP * ATTN_HEAD_DIM
ROPE_THETA = 500000.0
ROPE_DIM = ATTN_HEAD_DIM // 4
GLA_HEADS = 4
GLA_KEY_DIM = (D_MODEL // 2) // GLA_HEADS
GLA_VAL_DIM = D_MODEL // GLA_HEADS
GLA_GATE_RANK = 16
GLA_GATE_NORMALIZER = 16.0
GLA_CHUNK = 64
FFN_HIDDEN = -(-8 * D_MODEL // (3 * 256)) * 256
NORM_EPS = 1e-6

IN_SIZES = (ATTN_HEADS * ATTN_HEAD_DIM,
            ATTN_HEADS * ATTN_HEAD_DIM,
            ATTN_HEADS * ATTN_HEAD_DIM,
            GLA_HEADS * GLA_KEY_DIM,
            GLA_HEADS * GLA_KEY_DIM,
            GLA_HEADS * GLA_VAL_DIM,
            GLA_HEADS * GLA_VAL_DIM,
            GLA_GATE_RANK,
            D_MODEL,
            D_MODEL)
IN_WIDTH = int(sum(IN_SIZES))
IN_SPLIT_POINTS = tuple(int(v) for v in np.cumsum(IN_SIZES)[:-1])

kernel_name = "hybrid_dilated_swa_gla_block"


def rmsnorm(x, g):
    xf = x.astype(jnp.float32)
    y = xf * lax.rsqrt(jnp.mean(xf * xf, axis=-1, keepdims=True) + NORM_EPS)
    return (y * g.astype(jnp.float32)).astype(x.dtype)


def rope_tables(seq_len):
    inv_freq = ROPE_THETA ** (-jnp.arange(0, ROPE_DIM, 2, dtype=jnp.float32) / ROPE_DIM)
    ang = jnp.arange(seq_len, dtype=jnp.float32)[:, None] * inv_freq[None, :]
    ang = jnp.concatenate([ang, ang], axis=-1)[None, :, None, :]
    return jnp.cos(ang), jnp.sin(ang)


def partial_rope(t, cos, sin):
    rot = t[..., :ROPE_DIM].astype(jnp.float32)
    half = ROPE_DIM // 2
    rotated = jnp.concatenate([-rot[..., half:], rot[..., :half]], axis=-1)
    rot = rot * cos + rotated * sin
    return jnp.concatenate([rot.astype(t.dtype), t[..., ROPE_DIM:]], axis=-1)


def dilated_window_attention(q, k, v, dilation, keys_back):
    B, S, H, Dh = q.shape
    r, blk = dilation, keys_back
    L = S // r
    nb = -(-L // blk)
    Lp = nb * blk

    def to_sub(t):
        t = t.reshape(B, L, r, H, Dh).transpose(0, 2, 3, 1, 4)
        t = jnp.pad(t, ((0, 0), (0, 0), (0, 0), (0, Lp - L), (0, 0)))
        return t.reshape(B, r, H, nb, blk, Dh)

    def with_prev(t):
        prev = jnp.pad(t, ((0, 0), (0, 0), (0, 0), (1, 0), (0, 0), (0, 0)))[:, :, :, :-1]
        return jnp.concatenate([prev, t], axis=4)

    qb = to_sub(q)
    kk = with_prev(to_sub(k))
    vv = with_prev(to_sub(v))
    scores = jnp.einsum('brhnqd,brhnkd->brhnqk', qb, kk).astype(jnp.float32) * (Dh ** -0.5)
    qi = jnp.arange(blk)[:, None]
    kj = jnp.arange(2 * blk)[None, :] - blk
    dist = qi - kj
    blk_idx = jnp.arange(nb)[:, None, None]
    valid = (dist >= 0)[None] & (dist <= keys_back)[None] & ((blk_idx > 0) | (kj >= 0)[None])
    scores = jnp.where(valid, scores, -jnp.inf)
    m = jnp.max(scores, axis=-1)
    p = jnp.exp(scores - m[..., None])
    l = jnp.sum(p, axis=-1)
    o = jnp.einsum('brhnqk,brhnkd->brhnqd', p, vv.astype(jnp.float32)) / l[..., None]

    def from_sub(t):
        rest = t.shape[5:]
        t = t.reshape((B, r, H, Lp) + rest)[:, :, :, :L]
        t = t.transpose((0, 3, 1, 2) + tuple(range(4, 4 + len(rest))))
        return t.reshape((B, S, H) + rest)

    return from_sub(o), from_sub(m), from_sub(l)


def dilated_attention_mixer(q, k, v):
    B, S = q.shape[:2]
    outs, maxes, dens = [], [], []
    for g, (window, dilation) in enumerate(ATTN_GROUPS):
        sl = slice(g * ATTN_HEADS_PER_GROUP, (g + 1) * ATTN_HEADS_PER_GROUP)
        o, m, l = dilated_window_attention(q[:, :, sl], k[:, :, sl], v[:, :, sl],
                                           dilation, window // dilation)
        outs.append(o); maxes.append(m); dens.append(l)
    o = jnp.stack(outs, 0)
    m = jnp.stack(maxes, 0)
    l = jnp.stack(dens, 0)
    w = l * jnp.exp(m - jnp.max(m, axis=0, keepdims=True))
    out = jnp.sum(w[..., None] * o, axis=0) / jnp.sum(w, axis=0)[..., None]
    return out.reshape(B, S, ATTN_OUT_WIDTH)


def gla_chunked(q, k, v, log_a):
    B, S, H, dk = q.shape
    dv = v.shape[-1]
    C = GLA_CHUNK
    N = S // C
    q = q.reshape(B, N, C, H, dk) * (dk ** -0.5)
    k = k.reshape(B, N, C, H, dk)
    v = v.reshape(B, N, C, H, dv)
    b = jnp.cumsum(log_a.reshape(B, N, C, H, dk), axis=2)
    b_last = b[:, :, -1]
    b_ref = b[:, :, C // 2:C // 2 + 1]
    A = jnp.einsum('bnihd,bnjhd->bnhij', q * jnp.exp(b - b_ref), k * jnp.exp(b_ref - b))
    causal = jnp.tril(jnp.ones((C, C), dtype=bool))
    o_intra = jnp.einsum('bnhij,bnjhv->bnihv', jnp.where(causal, A, 0.0), v)
    q_inter = q * jnp.exp(b)
    k_state = k * jnp.exp(b_last[:, :, None] - b)
    decay = jnp.exp(b_last)

    def step(state, inp):
        qc, kc, vc, dc = inp
        o = jnp.einsum('bihd,bhdv->bihv', qc, state)
        state = dc[..., None] * state + jnp.einsum('bjhd,bjhv->bhdv', kc, vc)
        return state, o

    xs = (jnp.moveaxis(q_inter, 1, 0), jnp.moveaxis(k_state, 1, 0),
          jnp.moveaxis(v, 1, 0), jnp.moveaxis(decay, 1, 0))
    state0 = jnp.zeros((B, H, dk, dv), jnp.float32)
    _, o_inter = lax.scan(step, state0, xs)
    o = o_intra + jnp.moveaxis(o_inter, 0, 1)
    return o.reshape(B, S, H, dv)


def setup_inputs(seed: int = 0) -> dict:
    key = jax.random.key(seed)
    ks = jax.random.split(key, 16)
    f32 = jnp.float32
    nrm = lambda k, shape, fan_in: jax.random.normal(k, shape, f32) * (fan_in ** -0.5)
    gain = lambda k, shape: 1.0 + 0.01 * jax.random.normal(k, shape, f32)
    return {
        "x": jax.random.normal(ks[0], (BATCH, SEQ, D_MODEL), f32),
        "norm1_g": gain(ks[1], (DEPTH, D_MODEL)),
        "w_in": nrm(ks[2], (DEPTH, D_MODEL, IN_WIDTH), D_MODEL),
        "w_gate_lr": nrm(ks[3], (DEPTH, GLA_GATE_RANK, GLA_HEADS * GLA_KEY_DIM), GLA_GATE_RANK),
        "b_gate": 0.01 * jax.random.normal(ks[4], (DEPTH, GLA_HEADS * GLA_KEY_DIM), f32),
        "gla_norm_g": gain(ks[5], (DEPTH, GLA_VAL_DIM)),
        "w_branch_a": nrm(ks[6], (DEPTH, ATTN_OUT_WIDTH, D_MODEL), ATTN_OUT_WIDTH),
        "w_branch_b": nrm(ks[7], (DEPTH, GLA_HEADS * GLA_VAL_DIM, D_MODEL), GLA_HEADS * GLA_VAL_DIM),
        "w_out": nrm(ks[8], (DEPTH, D_MODEL, D_MODEL), D_MODEL),
        "norm2_g": gain(ks[9], (DEPTH, D_MODEL)),
        "w_ffn_in": nrm(ks[10], (DEPTH, D_MODEL, 2 * FFN_HIDDEN), D_MODEL),
        "w_ffn_down": nrm(ks[11], (DEPTH, FFN_HIDDEN, D_MODEL), FFN_HIDDEN),
        "norm_f_g": gain(ks[12], (D_MODEL,)),
    }


def reference(x, norm1_g, w_in, w_gate_lr, b_gate, gla_norm_g, w_branch_a, w_branch_b,
              w_out, norm2_g, w_ffn_in, w_ffn_down, norm_f_g):
    B, S, D = x.shape
    cos, sin = rope_tables(S)
    for layer in range(DEPTH):
        h = rmsnorm(x, norm1_g[layer])
        proj = h @ w_in[layer]
        aq, ak, av, gq, gk, gv, gr, glr, ga, gb = jnp.split(proj, IN_SPLIT_POINTS, axis=-1)

        aq = partial_rope(aq.reshape(B, S, ATTN_HEADS, ATTN_HEAD_DIM), cos, sin)
        ak = partial_rope(ak.reshape(B, S, ATTN_HEADS, ATTN_HEAD_DIM), cos, sin)
        av = av.reshape(B, S, ATTN_HEADS, ATTN_HEAD_DIM)
        attn_out = dilated_attention_mixer(aq, ak, av).astype(x.dtype)

        log_a = jax.nn.log_sigmoid((glr @ w_gate_lr[layer] + b_gate[layer]).astype(jnp.float32))
        log_a = (log_a / GLA_GATE_NORMALIZER).reshape(B, S, GLA_HEADS, GLA_KEY_DIM)
        o_gla = gla_chunked(gq.reshape(B, S, GLA_HEADS, GLA_KEY_DIM).astype(jnp.float32),
                            gk.reshape(B, S, GLA_HEADS, GLA_KEY_DIM).astype(jnp.float32),
                            gv.reshape(B, S, GLA_HEADS, GLA_VAL_DIM).astype(jnp.float32),
                            log_a)
        o_gla = rmsnorm(o_gla, gla_norm_g[layer])
        o_gla = o_gla * jax.nn.silu(gr.reshape(B, S, GLA_HEADS, GLA_VAL_DIM).astype(jnp.float32))
        gla_out = o_gla.reshape(B, S, GLA_HEADS * GLA_VAL_DIM).astype(x.dtype)

        mix = (jax.nn.sigmoid(ga) * (attn_out @ w_branch_a[layer])
               + jax.nn.sigmoid(gb) * (gla_out @ w_branch_b[layer]))
        x = x + mix @ w_out[layer]

        h2 = rmsnorm(x, norm2_g[layer])
        g, u = jnp.split(h2 @ w_ffn_in[layer], 2, axis=-1)
        x = x + (jax.nn.silu(g) * u) @ w_ffn_down[layer]
    return rmsnorm(x, norm_f_g)
```

```python
import functools

import jax
import jax.numpy as jnp
import numpy as np
from jax import lax
from jax.experimental import pallas as pl
from jax.experimental.pallas import tpu as pltpu

F32 = jnp.float32
BF16 = jnp.bfloat16

D_MODEL = 1024
ATTN_GROUPS = ((128, 1), (512, 4), (2048, 16))
HEADS_PER_GROUP = 4
HEAD_DIM = 128
GROUP_WIDTH = HEADS_PER_GROUP * HEAD_DIM
KEYS_BACK = 128
ROPE_THETA = 500000.0
ROPE_DIM = HEAD_DIM // 4
GLA_HEADS = 4
GLA_KEY_DIM = 128
GLA_VAL_DIM = 256
GLA_GATE_RANK = 16
GLA_GATE_NORMALIZER = 16.0
GLA_SUB = 64
FFN_HIDDEN = 2816
NORM_EPS = 1e-6

LANES = 128
VMEM_LIMIT_BYTES = 56 * 1024 * 1024

COL_TILE = 512
MAIN_WIDTH = 9728
N_COL_TILES = MAIN_WIDTH // COL_TILE
CT_GV, CT_GR, CT_GA, CT_GB = 0, 2, 4, 6
CT_AQ, CT_AK, CT_AV = 8, 11, 14
CT_GQ, CT_GK = 17, 18


def _params(semantics):
    return pltpu.CompilerParams(dimension_semantics=semantics, vmem_limit_bytes=VMEM_LIMIT_BYTES)


def _inproj_kernel(x_ref, g_ref, w_ref, wglr_ref, cos_ref, sina_ref, sinb_ref,
                   out_ref, glr_ref, h_sc):
    j = pl.program_id(1)

    @pl.when(j == 0)
    def _():
        x = x_ref[...]
        ms = jnp.mean(x * x, axis=-1, keepdims=True)
        h = (x * lax.rsqrt(ms + NORM_EPS) * g_ref[...]).astype(BF16)
        h_sc[...] = h
        glr_ref[...] = jnp.dot(h, wglr_ref[...], preferred_element_type=F32).astype(BF16)

    acc = jnp.dot(h_sc[...], w_ref[...], preferred_element_type=F32)
    is_rope = jnp.logical_and(j >= CT_AQ, j < CT_AV)

    @pl.when(is_rope)
    def _():
        c, sa, sb = cos_ref[...], sina_ref[...], sinb_ref[...]
        for hh in range(COL_TILE // HEAD_DIM):
            a = acc[:, hh * HEAD_DIM:(hh + 1) * HEAD_DIM]
            up = pltpu.roll(a, HEAD_DIM - ROPE_DIM // 2, 1)
            dn = pltpu.roll(a, ROPE_DIM // 2, 1)
            out_ref[:, hh * HEAD_DIM:(hh + 1) * HEAD_DIM] = (a * c + up * sa + dn * sb).astype(BF16)

    @pl.when(jnp.logical_not(is_rope))
    def _():
        out_ref[...] = acc.astype(BF16)


def _inproj(x2d, g1, w_main, w_glr, cos_t, sina_t, sinb_t, seq, tm):
    t = x2d.shape[0]
    pos_blocks = seq // tm
    table_spec = pl.BlockSpec((tm, LANES), lambda i, j: (i % pos_blocks, 0))
    return pl.pallas_call(
        _inproj_kernel,
        grid=(t // tm, N_COL_TILES),
        in_specs=[
            pl.BlockSpec((tm, D_MODEL), lambda i, j: (i, 0)),
            pl.BlockSpec((1, D_MODEL), lambda i, j: (0, 0)),
            pl.BlockSpec((D_MODEL, COL_TILE), lambda i, j: (0, j)),
            pl.BlockSpec((D_MODEL, LANES), lambda i, j: (0, 0)),
            table_spec, table_spec, table_spec,
        ],
        out_specs=[
            pl.BlockSpec((tm, COL_TILE), lambda i, j: (i, j)),
            pl.BlockSpec((tm, LANES), lambda i, j: (i, 0)),
        ],
        out_shape=[
            jax.ShapeDtypeStruct((t, MAIN_WIDTH), BF16),
            jax.ShapeDtypeStruct((t, LANES), BF16),
        ],
        scratch_shapes=[pltpu.VMEM((tm, D_MODEL), BF16)],
        compiler_params=_params(("parallel", "arbitrary")),
        name="inproj",
    )(x2d, g1, w_main, w_glr, cos_t, sina_t, sinb_t)


def _attn_kernel(q_ref, kp_ref, kc_ref, vp_ref, vc_ref, o_ref, st_ref, *, tq):
    n = pl.program_id(2)
    blk = KEYS_BACK
    qi = lax.broadcasted_iota(jnp.int32, (blk, 2 * blk), 0)
    jj = lax.broadcasted_iota(jnp.int32, (blk, 2 * blk), 1)
    band = jnp.logical_and(jj >= qi, jj <= qi + blk)
    first_band = jnp.logical_and(band, jj + n * tq >= blk)
    lane = lax.broadcasted_iota(jnp.int32, (blk, LANES), 1)
    scale = HEAD_DIM ** -0.5
    neg_inf = jnp.float32(-jnp.inf)

    for qb in range(tq // blk):
        rows = slice(qb * blk, (qb + 1) * blk)
        valid = first_band if qb == 0 else band
        stats = jnp.zeros((blk, LANES), F32)
        for h in range(HEADS_PER_GROUP):
            hs = slice(h * HEAD_DIM, (h + 1) * HEAD_DIM)
            q = q_ref[0, rows, hs]
            if qb == 0:
                kk = jnp.concatenate([kp_ref[0, :, hs], kc_ref[0, 0:blk, hs]], axis=0)
                vv = jnp.concatenate([vp_ref[0, :, hs], vc_ref[0, 0:blk, hs]], axis=0)
            else:
                kk = kc_ref[0, (qb - 1) * blk:(qb + 1) * blk, hs]
                vv = vc_ref[0, (qb - 1) * blk:(qb + 1) * blk, hs]
            s = lax.dot_general(q, kk, (((1,), (1,)), ((), ())), preferred_element_type=F32)
            s = jnp.where(valid, s * scale, neg_inf)
            m = jnp.max(s, axis=-1, keepdims=True)
            p = jnp.exp(s - m)
            l = jnp.sum(p, axis=-1, keepdims=True)
            o = jnp.dot(p.astype(BF16), vv, preferred_element_type=F32) / l
            o_ref[0, rows, hs] = o.astype(BF16)
            stats = jnp.where(lane == h, m, stats)
            stats = jnp.where(lane == HEADS_PER_GROUP + h, l, stats)
        st_ref[0, rows, :] = stats


def _attention_group(proj, group, dilation, batch, seq, tq):
    r = dilation
    sub_len = seq // r
    tq = min(tq, sub_len)
    nblk = sub_len // tq
    prev_per_blk = tq // KEYS_BACK
    proj_v = proj.reshape(batch, sub_len, r * MAIN_WIDTH)
    tiles_per_pos = N_COL_TILES

    def col(base):
        return lambda b, c, n: (b, n, c * tiles_per_pos + base + group)

    def col_prev(base):
        return lambda b, c, n: (b, jnp.maximum(n * prev_per_blk - 1, 0), c * tiles_per_pos + base + group)

    o, st = pl.pallas_call(
        functools.partial(_attn_kernel, tq=tq),
        grid=(batch, r, nblk),
        in_specs=[
            pl.BlockSpec((1, tq, GROUP_WIDTH), col(CT_AQ)),
            pl.BlockSpec((1, KEYS_BACK, GROUP_WIDTH), col_prev(CT_AK)),
            pl.BlockSpec((1, tq, GROUP_WIDTH), col(CT_AK)),
            pl.BlockSpec((1, KEYS_BACK, GROUP_WIDTH), col_prev(CT_AV)),
            pl.BlockSpec((1, tq, GROUP_WIDTH), col(CT_AV)),
        ],
        out_specs=[
            pl.BlockSpec((1, tq, GROUP_WIDTH), lambda b, c, n: (b, n, c)),
            pl.BlockSpec((1, tq, LANES), lambda b, c, n: (b, n, c)),
        ],
        out_shape=[
            jax.ShapeDtypeStruct((batch, sub_len, r * GROUP_WIDTH), BF16),
            jax.ShapeDtypeStruct((batch, sub_len, r * LANES), F32),
        ],
        compiler_params=_params(("parallel", "parallel", "arbitrary")),
        name=f"attn_r{r}",
    )(proj_v, proj_v, proj_v, proj_v, proj_v)
    return o.reshape(batch * seq, GROUP_WIDTH), st.reshape(batch * seq, LANES)


def _gla_kernel(q_ref, k_ref, v_ref, gr_ref, glr_ref, wg_ref, bg_ref, gn_ref, tri_ref,
                o_ref, state_sc, *, tc, chunk):
    n = pl.program_id(1)

    @pl.when(n == 0)
    def _():
        state_sc[...] = jnp.zeros_like(state_sc)

    z = jnp.dot(glr_ref[...], wg_ref[...], preferred_element_type=F32) + bg_ref[...]
    log_a = (jnp.minimum(z, 0.0) - jnp.log(1.0 + jnp.exp(-jnp.abs(z)))) / GLA_GATE_NORMALIZER
    hi = log_a.astype(BF16)
    lo = (log_a - hi.astype(F32)).astype(BF16)
    tri = tri_ref[...]
    bcum = (jnp.dot(tri, hi, preferred_element_type=F32)
            + jnp.dot(tri, lo, preferred_element_type=F32))

    sub = GLA_SUB
    nsub = chunk // sub
    sub_shift = sub.bit_length() - 1
    row = lax.broadcasted_iota(jnp.int32, (chunk, chunk), 0)
    colm = lax.broadcasted_iota(jnp.int32, (chunk, chunk), 1)
    diag_mask = jnp.logical_and((row >> sub_shift) == (colm >> sub_shift), colm <= row)
    rsub = lax.broadcasted_iota(jnp.int32, (chunk, GLA_KEY_DIM), 0) >> sub_shift
    q_scale = GLA_KEY_DIM ** -0.5
    dn_t = (((1,), (1,)), ((), ()))
    dn_l = (((0,), (0,)), ((), ()))

    for h in range(GLA_HEADS):
        ks = slice(h * GLA_KEY_DIM, (h + 1) * GLA_KEY_DIM)
        vs = slice(h * GLA_VAL_DIM, (h + 1) * GLA_VAL_DIM)
        for c in range(tc // chunk):
            rows = slice(c * chunk, (c + 1) * chunk)
            b = bcum[rows, ks]
            q = q_ref[rows, ks].astype(F32) * q_scale
            k = k_ref[rows, ks].astype(F32)
            v = v_ref[rows, vs]
            centre = b[sub // 2:sub // 2 + 1, :]
            for sb in range(1, nsub):
                centre = jnp.where(rsub == sb, b[sb * sub + sub // 2:sb * sub + sub // 2 + 1, :], centre)
            qc = (q * jnp.exp(b - centre)).astype(BF16)
            kc = (k * jnp.exp(centre - b)).astype(BF16)
            a = jnp.where(diag_mask, lax.dot_general(qc, kc, dn_t, preferred_element_type=F32), 0.0)
            for sb in range(1, nsub):
                lo_r, hi_r = sb * sub, (sb + 1) * sub
                bound = b[lo_r - 1:lo_r, :]
                qb = (q[lo_r:hi_r] * jnp.exp(b[lo_r:hi_r] - bound)).astype(BF16)
                kb = (k[0:lo_r] * jnp.exp(bound - b[0:lo_r])).astype(BF16)
                kb = jnp.concatenate([kb, jnp.zeros((chunk - lo_r, GLA_KEY_DIM), BF16)], axis=0)
                off = lax.dot_general(qb, kb, dn_t, preferred_element_type=F32)
                pieces = []
                if lo_r:
                    pieces.append(jnp.zeros((lo_r, chunk), F32))
                pieces.append(off)
                if chunk - hi_r:
                    pieces.append(jnp.zeros((chunk - hi_r, chunk), F32))
                a = a + jnp.concatenate(pieces, axis=0)
            b_last = b[chunk - 1:chunk, :]
            q_in = (q * jnp.exp(b)).astype(BF16)
            k_st = (k * jnp.exp(b_last - b)).astype(BF16)
            st = state_sc[h]
            o = (jnp.dot(a.astype(BF16), v, preferred_element_type=F32)
                 + lax.dot_general(q_in, st.astype(BF16), dn_t, preferred_element_type=F32))
            state_sc[h] = jnp.exp(b_last) * st + lax.dot_general(v, k_st, dn_l,
                                                                   preferred_element_type=F32)
            ms = jnp.mean(o * o, axis=-1, keepdims=True)
            y = o * lax.rsqrt(ms + NORM_EPS) * gn_ref[...]
            g = gr_ref[rows, vs].astype(F32)
            o_ref[rows, vs] = (y * (g * jax.nn.sigmoid(g))).astype(BF16)


def _gla(proj, glr, wg, bg, gn, tri, batch, seq, tc, chunk):
    nblk = seq // tc
    row = lambda b, n: b * nblk + n
    return pl.pallas_call(
        functools.partial(_gla_kernel, tc=tc, chunk=chunk),
        grid=(batch, nblk),
        in_specs=[
            pl.BlockSpec((tc, COL_TILE), lambda b, n: (row(b, n), CT_GQ)),
            pl.BlockSpec((tc, COL_TILE), lambda b, n: (row(b, n), CT_GK)),
            pl.BlockSpec((tc, 2 * COL_TILE), lambda b, n: (row(b, n), CT_GV // 2)),
            pl.BlockSpec((tc, 2 * COL_TILE), lambda b, n: (row(b, n), CT_GR // 2)),
            pl.BlockSpec((tc, LANES), lambda b, n: (row(b, n), 0)),
            pl.BlockSpec((LANES, GLA_HEADS * GLA_KEY_DIM), lambda b, n: (0, 0)),
            pl.BlockSpec((1, GLA_HEADS * GLA_KEY_DIM), lambda b, n: (0, 0)),
            pl.BlockSpec((1, GLA_VAL_DIM), lambda b, n: (0, 0)),
            pl.BlockSpec((tc, tc), lambda b, n: (0, 0)),
        ],
        out_specs=pl.BlockSpec((tc, GLA_HEADS * GLA_VAL_DIM), lambda b, n: (row(b, n), 0)),
        out_shape=jax.ShapeDtypeStruct((batch * seq, GLA_HEADS * GLA_VAL_DIM), BF16),
        scratch_shapes=[pltpu.VMEM((GLA_HEADS, GLA_VAL_DIM, GLA_KEY_DIM), F32)],
        compiler_params=_params(("parallel", "arbitrary")),
        name="gla",
    )(proj, proj, proj, proj, glr, wg, bg, gn, tri)


def _rms(x, g):
    ms = jnp.mean(x * x, axis=-1, keepdims=True)
    return x * lax.rsqrt(ms + NORM_EPS) * g


def _post_kernel(x_ref, o1_ref, o2_ref, o3_ref, s1_ref, s2_ref, s3_ref, gla_ref, ga_ref, gb_ref,
                 wa_ref, wb_ref, wo_ref, g2_ref, wi_ref, wd_ref, gf_ref, out_ref, *, ffn_chunk):
    o_refs = (o1_ref, o2_ref, o3_ref)
    stats = [s[...] for s in (s1_ref, s2_ref, s3_ref)]
    heads = []
    for h in range(HEADS_PER_GROUP):
        hs = slice(h * HEAD_DIM, (h + 1) * HEAD_DIM)
        ms = [s[:, h:h + 1] for s in stats]
        ls = [s[:, HEADS_PER_GROUP + h:HEADS_PER_GROUP + h + 1] for s in stats]
        m_all = jnp.maximum(jnp.maximum(ms[0], ms[1]), ms[2])
        ws = [l * jnp.exp(m - m_all) for m, l in zip(ms, ls)]
        inv = 1.0 / (ws[0] + ws[1] + ws[2])
        acc = (ws[0] * inv) * o_refs[0][:, hs].astype(F32)
        acc = acc + (ws[1] * inv) * o_refs[1][:, hs].astype(F32)
        acc = acc + (ws[2] * inv) * o_refs[2][:, hs].astype(F32)
        heads.append(acc.astype(BF16))
    attn = jnp.concatenate(heads, axis=1)

    ya = jnp.dot(attn, wa_ref[...], preferred_element_type=F32)
    yb = jnp.dot(gla_ref[...], wb_ref[...], preferred_element_type=F32)
    mix = (jax.nn.sigmoid(ga_ref[...].astype(F32)) * ya
           + jax.nn.sigmoid(gb_ref[...].astype(F32)) * yb)
    x1 = x_ref[...] + jnp.dot(mix.astype(BF16), wo_ref[...], preferred_element_type=F32)

    h2 = _rms(x1, g2_ref[...]).astype(BF16)
    acc = x1
    for c in range(FFN_HIDDEN // ffn_chunk):
        gcols = slice(c * ffn_chunk, (c + 1) * ffn_chunk)
        ucols = slice(FFN_HIDDEN + c * ffn_chunk, FFN_HIDDEN + (c + 1) * ffn_chunk)
        g = jnp.dot(h2, wi_ref[:, gcols], preferred_element_type=F32)
        u = jnp.dot(h2, wi_ref[:, ucols], preferred_element_type=F32)
        a = (g * jax.nn.sigmoid(g) * u).astype(BF16)
        acc = acc + jnp.dot(a, wd_ref[gcols, :], preferred_element_type=F32)
    out_ref[...] = _rms(acc, gf_ref[...])


def _post(x2d, attn_o, attn_st, gla_out, proj, wa, wb, wo, g2, wi, wd, gf, tm, ffn_chunk):
    t = x2d.shape[0]
    resident = lambda shape: pl.BlockSpec(shape, lambda i: (0, 0), pipeline_mode=pl.Buffered(1))
    rows = lambda width, col=0: pl.BlockSpec((tm, width), lambda i: (i, col))
    return pl.pallas_call(
        functools.partial(_post_kernel, ffn_chunk=ffn_chunk),
        grid=(t // tm,),
        in_specs=[
            rows(D_MODEL),
            rows(GROUP_WIDTH), rows(GROUP_WIDTH), rows(GROUP_WIDTH),
            rows(LANES), rows(LANES), rows(LANES),
            rows(D_MODEL),
            rows(D_MODEL, CT_GA // 2), rows(D_MODEL, CT_GB // 2),
            resident(wa.shape), resident(wb.shape), resident(wo.shape), resident(g2.shape),
            resident(wi.shape), resident(wd.shape), resident(gf.shape),
        ],
        out_specs=rows(D_MODEL),
        out_shape=jax.ShapeDtypeStruct((t, D_MODEL), F32),
        compiler_params=_params(("parallel",)),
        name="post",
    )(x2d, *attn_o, *attn_st, gla_out, proj, proj, wa, wb, wo, g2, wi, wd, gf)


def _rope_tables(seq):
    half = ROPE_DIM // 2
    inv_freq = ROPE_THETA ** (-jnp.arange(0, ROPE_DIM, 2, dtype=F32) / ROPE_DIM)
    ang = jnp.arange(seq, dtype=F32)[:, None] * inv_freq[None, :]
    ang = jnp.concatenate([ang, ang], axis=-1)
    cos, sin = jnp.cos(ang), jnp.sin(ang)
    cos_t = jnp.concatenate([cos, jnp.ones((seq, HEAD_DIM - ROPE_DIM), F32)], axis=-1)
    sina_t = jnp.concatenate([-sin[:, :half], jnp.zeros((seq, HEAD_DIM - half), F32)], axis=-1)
    sinb_t = jnp.concatenate([jnp.zeros((seq, half), F32), sin[:, half:],
                              jnp.zeros((seq, HEAD_DIM - ROPE_DIM), F32)], axis=-1)
    return cos_t, sina_t, sinb_t


def _block_tril(n, blk):
    r = np.arange(n)
    return jnp.asarray((r[:, None] // blk == r[None, :] // blk) & (r[None, :] <= r[:, None]), BF16)


def _layer(x2d, batch, seq, norm1_g, w_in, w_gate_lr, b_gate, gla_norm_g, w_branch_a, w_branch_b,
           w_out, norm2_g, w_ffn_in, w_ffn_down, out_g, tables):
    a_w = 3 * GROUP_WIDTH
    gk_w = GLA_HEADS * GLA_KEY_DIM
    gv_w = GLA_HEADS * GLA_VAL_DIM
    o_aq, o_ak, o_av = 0, a_w, 2 * a_w
    o_gq = 3 * a_w
    o_gk = o_gq + gk_w
    o_gv = o_gk + gk_w
    o_gr = o_gv + gv_w
    o_glr = o_gr + gv_w
    o_ga = o_glr + GLA_GATE_RANK
    o_gb = o_ga + D_MODEL
    cols = lambda o, w: w_in[:, o:o + w]
    w_main = jnp.concatenate(
        [cols(o_gv, gv_w), cols(o_gr, gv_w), cols(o_ga, D_MODEL), cols(o_gb, D_MODEL),
         cols(o_aq, a_w), cols(o_ak, a_w), cols(o_av, a_w), cols(o_gq, gk_w), cols(o_gk, gk_w)],
        axis=1).astype(BF16)
    w_glr = jnp.pad(cols(o_glr, GLA_GATE_RANK), ((0, 0), (0, LANES - GLA_GATE_RANK))).astype(BF16)
    wg = jnp.pad(w_gate_lr, ((0, LANES - GLA_GATE_RANK), (0, 0))).astype(BF16)

    proj, glr = _inproj(x2d, norm1_g[None, :], w_main, w_glr, *tables, seq=seq, tm=1024)

    attn_o, attn_st = [], []
    for g, (_, dilation) in enumerate(ATTN_GROUPS):
        o, st = _attention_group(proj, g, dilation, batch, seq, tq=512)
        attn_o.append(o)
        attn_st.append(st)

    tc = 256
    gla_out = _gla(proj, glr, wg, b_gate[None, :], gla_norm_g[None, :], _block_tril(tc, 128),
                   batch, seq, tc=tc, chunk=128)

    return _post(x2d, attn_o, attn_st, gla_out, proj,
                 w_branch_a.astype(BF16), w_branch_b.astype(BF16), w_out.astype(BF16),
                 norm2_g[None, :], w_ffn_in.astype(BF16), w_ffn_down.astype(BF16), out_g[None, :],
                 tm=256, ffn_chunk=1408)


def kernel(x, norm1_g, w_in, w_gate_lr, b_gate, gla_norm_g, w_branch_a, w_branch_b, w_out, norm2_g,
           w_ffn_in, w_ffn_down, norm_f_g):
    batch, seq, d = x.shape
    depth = w_in.shape[0]
    assert depth == 1 and d == D_MODEL
    tables = _rope_tables(seq)
    x2d = x.reshape(batch * seq, d)
    out = _layer(x2d, batch, seq, norm1_g[0], w_in[0], w_gate_lr[0], b_gate[0], gla_norm_g[0],
                 w_branch_a[0], w_branch_b[0], w_out[0], norm2_g[0], w_ffn_in[0], w_ffn_down[0],
                 norm_f_g, tables)
    return out.reshape(batch, seq, d)
```

```python
import functools

import jax
import jax.numpy as jnp
import numpy as np
from jax import lax
from jax.experimental import pallas as pl
from jax.experimental.pallas import tpu as pltpu

F32 = jnp.float32
BF16 = jnp.bfloat16

D_MODEL = 1024
ATTN_GROUPS = ((128, 1), (512, 4), (2048, 16))
HEADS_PER_GROUP = 4
HEAD_DIM = 128
GROUP_WIDTH = HEADS_PER_GROUP * HEAD_DIM
KEYS_BACK = 128
ROPE_THETA = 500000.0
ROPE_DIM = HEAD_DIM // 4
GLA_HEADS = 4
GLA_KEY_DIM = 128
GLA_VAL_DIM = 256
GLA_GATE_RANK = 16
GLA_GATE_NORMALIZER = 16.0
GLA_SUB = 64
FFN_HIDDEN = 2816
NORM_EPS = 1e-6

LANES = 128
VMEM_LIMIT_BYTES = 56 * 1024 * 1024

COL_TILE = 512
CT_GV, CT_GR, CT_GA, CT_GB = 0, 2, 4, 6
CT_Q1, CT_K1, CT_V1 = 8, 9, 10
CT_GQ, CT_GK = 11, 12
N_MAIN_TILES = 13
MAIN_WIDTH = N_MAIN_TILES * COL_TILE
CT_G2, CT_G3 = 13, 16
N_COL_TILES = 19
QKV_WIDTH = 3 * COL_TILE


def _params(semantics):
    return pltpu.CompilerParams(dimension_semantics=semantics, vmem_limit_bytes=VMEM_LIMIT_BYTES)


def _inproj_kernel(x_ref, g_ref, w_ref, wglr_ref, cos_ref, sina_ref, sinb_ref,
                   out_ref, glr_ref, d2_ref, d3_ref, h_sc, slab_sc, *, tm):
    j = pl.program_id(1)

    @pl.when(j == 0)
    def _():
        x = x_ref[...]
        ms = jnp.mean(x * x, axis=-1, keepdims=True)
        h = (x * lax.rsqrt(ms + NORM_EPS) * g_ref[...]).astype(BF16)
        h_sc[...] = h
        glr_ref[...] = jnp.dot(h, wglr_ref[...], preferred_element_type=F32).astype(BF16)

    acc = jnp.dot(h_sc[...], w_ref[...], preferred_element_type=F32)

    def finish(rope, dilation, dst_ref):
        for hh in range(COL_TILE // HEAD_DIM):
            hs = slice(hh * HEAD_DIM, (hh + 1) * HEAD_DIM)
            a = acc[:, hs]
            if rope:
                up = pltpu.roll(a, HEAD_DIM - ROPE_DIM // 2, 1)
                dn = pltpu.roll(a, ROPE_DIM // 2, 1)
                a = a * cos_ref[...] + up * sina_ref[...] + dn * sinb_ref[...]
            if dilation == 1:
                dst_ref[:, hs] = a.astype(BF16)
            else:
                slab_sc[hh] = a
                for c in range(dilation):
                    run = slab_sc[hh, pl.ds(c, tm // dilation, stride=dilation), :]
                    dst_ref[0, c, :, hs] = run.astype(BF16)

    def case(cond, rope, dilation, dst_ref):
        pl.when(cond)(functools.partial(finish, rope, dilation, dst_ref))

    is_qk1 = jnp.logical_or(j == CT_Q1, j == CT_K1)
    case(is_qk1, True, 1, out_ref)
    case(jnp.logical_and(j < CT_G2, jnp.logical_not(is_qk1)), False, 1, out_ref)
    case(jnp.logical_and(j >= CT_G2, j < CT_G2 + 2), True, 4, d2_ref)
    case(j == CT_G2 + 2, False, 4, d2_ref)
    case(jnp.logical_and(j >= CT_G3, j < CT_G3 + 2), True, 16, d3_ref)
    case(j == CT_G3 + 2, False, 16, d3_ref)


def _inproj(x2d, g1, w_main, w_glr, cos_t, sina_t, sinb_t, batch, seq, tm):
    t = x2d.shape[0]
    pos_blocks = seq // tm
    table_spec = pl.BlockSpec((tm, LANES), lambda i, j: (i % pos_blocks, 0))

    def dilated_spec(dilation, first_tile):
        return pl.BlockSpec(
            (1, dilation, tm // dilation, COL_TILE),
            lambda i, j: (i // pos_blocks, 0, i % pos_blocks, jnp.clip(j - first_tile, 0, 2)))

    return pl.pallas_call(
        functools.partial(_inproj_kernel, tm=tm),
        grid=(t // tm, N_COL_TILES),
        in_specs=[
            pl.BlockSpec((tm, D_MODEL), lambda i, j: (i, 0)),
            pl.BlockSpec((1, D_MODEL), lambda i, j: (0, 0)),
            pl.BlockSpec((D_MODEL, COL_TILE), lambda i, j: (0, j)),
            pl.BlockSpec((D_MODEL, LANES), lambda i, j: (0, 0)),
            table_spec, table_spec, table_spec,
        ],
        out_specs=[
            pl.BlockSpec((tm, COL_TILE), lambda i, j: (i, jnp.minimum(j, N_MAIN_TILES - 1))),
            pl.BlockSpec((tm, LANES), lambda i, j: (i, 0)),
            dilated_spec(4, CT_G2),
            dilated_spec(16, CT_G3),
        ],
        out_shape=[
            jax.ShapeDtypeStruct((t, MAIN_WIDTH), BF16),
            jax.ShapeDtypeStruct((t, LANES), BF16),
            jax.ShapeDtypeStruct((batch, 4, seq // 4, QKV_WIDTH), BF16),
            jax.ShapeDtypeStruct((batch, 16, seq // 16, QKV_WIDTH), BF16),
        ],
        scratch_shapes=[pltpu.VMEM((tm, D_MODEL), BF16),
                        pltpu.VMEM((COL_TILE // HEAD_DIM, tm, HEAD_DIM), F32)],
        compiler_params=_params(("parallel", "arbitrary")),
        name="inproj",
    )(x2d, g1, w_main, w_glr, cos_t, sina_t, sinb_t)


def _attn_kernel(q_ref, kp_ref, kc_ref, vp_ref, vc_ref, o_ref, st_ref, *scratch, tq, dilation):
    n = pl.program_id(1)
    blk = KEYS_BACK
    qi = lax.broadcasted_iota(jnp.int32, (blk, 2 * blk), 0)
    jj = lax.broadcasted_iota(jnp.int32, (blk, 2 * blk), 1)
    band = jnp.logical_and(jj >= qi, jj <= qi + blk)
    first_band = jnp.logical_and(band, jj + n * tq >= blk)
    lane = lax.broadcasted_iota(jnp.int32, (blk, LANES), 1)
    scale = HEAD_DIM ** -0.5
    neg_inf = jnp.float32(-jnp.inf)

    def phase(c):
        for qb in range(tq // blk):
            rows = slice(qb * blk, (qb + 1) * blk)
            valid = first_band if qb == 0 else band
            stats = jnp.zeros((blk, LANES), F32)
            nat = pl.ds(c + qb * blk * dilation, blk, stride=dilation)
            for h in range(HEADS_PER_GROUP):
                hs = slice(h * HEAD_DIM, (h + 1) * HEAD_DIM)
                q = q_ref[0, c, rows, hs]
                if qb == 0:
                    kk = jnp.concatenate([kp_ref[0, c, :, hs], kc_ref[0, c, 0:blk, hs]], axis=0)
                    vv = jnp.concatenate([vp_ref[0, c, :, hs], vc_ref[0, c, 0:blk, hs]], axis=0)
                else:
                    kk = kc_ref[0, c, (qb - 1) * blk:(qb + 1) * blk, hs]
                    vv = vc_ref[0, c, (qb - 1) * blk:(qb + 1) * blk, hs]
                s = lax.dot_general(q, kk, (((1,), (1,)), ((), ())), preferred_element_type=F32)
                s = jnp.where(valid, s * scale, neg_inf)
                m = jnp.max(s, axis=-1, keepdims=True)
                p = jnp.exp(s - m)
                l = jnp.sum(p, axis=-1, keepdims=True)
                o = jnp.dot(p.astype(BF16), vv, preferred_element_type=F32) / l
                if dilation == 1:
                    o_ref[0, rows, hs] = o.astype(BF16)
                else:
                    scratch[0][h, nat, :] = o
                stats = jnp.where(lane == h, m, stats)
                stats = jnp.where(lane == HEADS_PER_GROUP + h, l, stats)
            if dilation == 1:
                st_ref[0, rows, :] = stats
            else:
                st_ref[0, nat, :] = stats

    if dilation == 1:
        phase(0)
    else:
        def body(c, carry):
            phase(c)
            return carry
        lax.fori_loop(0, dilation, body, 0)
        for h in range(HEADS_PER_GROUP):
            o_ref[0, :, h * HEAD_DIM:(h + 1) * HEAD_DIM] = scratch[0][h].astype(BF16)


def _attention_group(qkv, col_tiles, dilation, batch, seq, tile_positions):
    r = dilation
    tq = tile_positions // r
    nblk = seq // tile_positions
    prev_per_blk = tq // KEYS_BACK
    cq, ck, cv = col_tiles

    def cur(ct):
        return pl.BlockSpec((1, r, tq, GROUP_WIDTH), lambda b, n: (b, 0, n, ct))

    def prev(ct):
        return pl.BlockSpec((1, r, KEYS_BACK, GROUP_WIDTH),
                            lambda b, n: (b, 0, jnp.maximum(n * prev_per_blk - 1, 0), ct))

    scratch = []
    if r > 1:
        scratch.append(pltpu.VMEM((HEADS_PER_GROUP, tile_positions, HEAD_DIM), F32))
    o, st = pl.pallas_call(
        functools.partial(_attn_kernel, tq=tq, dilation=r),
        grid=(batch, nblk),
        in_specs=[cur(cq), prev(ck), cur(ck), prev(cv), cur(cv)],
        out_specs=[
            pl.BlockSpec((1, tile_positions, GROUP_WIDTH), lambda b, n: (b, n, 0)),
            pl.BlockSpec((1, tile_positions, LANES), lambda b, n: (b, n, 0)),
        ],
        out_shape=[
            jax.ShapeDtypeStruct((batch, seq, GROUP_WIDTH), BF16),
            jax.ShapeDtypeStruct((batch, seq, LANES), F32),
        ],
        scratch_shapes=scratch,
        compiler_params=_params(("parallel", "arbitrary")),
        name=f"attn_r{r}",
    )(qkv, qkv, qkv, qkv, qkv)
    return o.reshape(batch * seq, GROUP_WIDTH), st.reshape(batch * seq, LANES)


def _gla_kernel(q_ref, k_ref, v_ref, gr_ref, glr_ref, wg_ref, bg_ref, gn_ref, tri_ref,
                o_ref, state_sc, *, tc, chunk):
    n = pl.program_id(1)

    @pl.when(n == 0)
    def _():
        state_sc[...] = jnp.zeros_like(state_sc)

    z = jnp.dot(glr_ref[...], wg_ref[...], preferred_element_type=F32) + bg_ref[...]
    log_a = (jnp.minimum(z, 0.0) - jnp.log(1.0 + jnp.exp(-jnp.abs(z)))) / GLA_GATE_NORMALIZER
    hi = log_a.astype(BF16)
    lo = (log_a - hi.astype(F32)).astype(BF16)
    tri = tri_ref[...]
    bcum = (jnp.dot(tri, hi, preferred_element_type=F32)
            + jnp.dot(tri, lo, preferred_element_type=F32))

    sub = GLA_SUB
    nsub = chunk // sub
    sub_shift = sub.bit_length() - 1
    row = lax.broadcasted_iota(jnp.int32, (chunk, chunk), 0)
    colm = lax.broadcasted_iota(jnp.int32, (chunk, chunk), 1)
    diag_mask = jnp.logical_and((row >> sub_shift) == (colm >> sub_shift), colm <= row)
    rsub = lax.broadcasted_iota(jnp.int32, (chunk, GLA_KEY_DIM), 0) >> sub_shift
    q_scale = GLA_KEY_DIM ** -0.5
    dn_t = (((1,), (1,)), ((), ()))
    dn_l = (((0,), (0,)), ((), ()))

    for h in range(GLA_HEADS):
        ks = slice(h * GLA_KEY_DIM, (h + 1) * GLA_KEY_DIM)
        vs = slice(h * GLA_VAL_DIM, (h + 1) * GLA_VAL_DIM)
        for c in range(tc // chunk):
            rows = slice(c * chunk, (c + 1) * chunk)
            b = bcum[rows, ks]
            q = q_ref[rows, ks].astype(F32) * q_scale
            k = k_ref[rows, ks].astype(F32)
            v = v_ref[rows, vs]
            centre = b[sub // 2:sub // 2 + 1, :]
            for sb in range(1, nsub):
                centre = jnp.where(rsub == sb, b[sb * sub + sub // 2:sb * sub + sub // 2 + 1, :], centre)
            qc = (q * jnp.exp(b - centre)).astype(BF16)
            kc = (k * jnp.exp(centre - b)).astype(BF16)
            a = jnp.where(diag_mask, lax.dot_general(qc, kc, dn_t, preferred_element_type=F32), 0.0)
            for sb in range(1, nsub):
                lo_r, hi_r = sb * sub, (sb + 1) * sub
                bound = b[lo_r - 1:lo_r, :]
                qb = (q[lo_r:hi_r] * jnp.exp(b[lo_r:hi_r] - bound)).astype(BF16)
                kb = (k[0:lo_r] * jnp.exp(bound - b[0:lo_r])).astype(BF16)
                kb = jnp.concatenate([kb, jnp.zeros((chunk - lo_r, GLA_KEY_DIM), BF16)], axis=0)
                off = lax.dot_general(qb, kb, dn_t, preferred_element_type=F32)
                pieces = []
                if lo_r:
                    pieces.append(jnp.zeros((lo_r, chunk), F32))
                pieces.append(off)
                if chunk - hi_r:
                    pieces.append(jnp.zeros((chunk - hi_r, chunk), F32))
                a = a + jnp.concatenate(pieces, axis=0)
            b_last = b[chunk - 1:chunk, :]
            q_in = (q * jnp.exp(b)).astype(BF16)
            k_st = (k * jnp.exp(b_last - b)).astype(BF16)
            st = state_sc[h]
            o = (jnp.dot(a.astype(BF16), v, preferred_element_type=F32)
                 + lax.dot_general(q_in, st.astype(BF16), dn_t, preferred_element_type=F32))
            state_sc[h] = jnp.exp(b_last) * st + lax.dot_general(v, k_st, dn_l,
                                                                   preferred_element_type=F32)
            ms = jnp.mean(o * o, axis=-1, keepdims=True)
            y = o * lax.rsqrt(ms + NORM_EPS) * gn_ref[...]
            g = gr_ref[rows, vs].astype(F32)
            o_ref[rows, vs] = (y * (g * jax.nn.sigmoid(g))).astype(BF16)


def _gla(proj, glr, wg, bg, gn, tri, batch, seq, tc, chunk):
    nblk = seq // tc
    row = lambda b, n: b * nblk + n
    return pl.pallas_call(
        functools.partial(_gla_kernel, tc=tc, chunk=chunk),
        grid=(batch, nblk),
        in_specs=[
            pl.BlockSpec((tc, COL_TILE), lambda b, n: (row(b, n), CT_GQ)),
            pl.BlockSpec((tc, COL_TILE), lambda b, n: (row(b, n), CT_GK)),
            pl.BlockSpec((tc, 2 * COL_TILE), lambda b, n: (row(b, n), CT_GV // 2)),
            pl.BlockSpec((tc, 2 * COL_TILE), lambda b, n: (row(b, n), CT_GR // 2)),
            pl.BlockSpec((tc, LANES), lambda b, n: (row(b, n), 0)),
            pl.BlockSpec((LANES, GLA_HEADS * GLA_KEY_DIM), lambda b, n: (0, 0)),
            pl.BlockSpec((1, GLA_HEADS * GLA_KEY_DIM), lambda b, n: (0, 0)),
            pl.BlockSpec((1, GLA_VAL_DIM), lambda b, n: (0, 0)),
            pl.BlockSpec((tc, tc), lambda b, n: (0, 0)),
        ],
        out_specs=pl.BlockSpec((tc, GLA_HEADS * GLA_VAL_DIM), lambda b, n: (row(b, n), 0)),
        out_shape=jax.ShapeDtypeStruct((batch * seq, GLA_HEADS * GLA_VAL_DIM), BF16),
        scratch_shapes=[pltpu.VMEM((GLA_HEADS, GLA_VAL_DIM, GLA_KEY_DIM), F32)],
        compiler_params=_params(("parallel", "arbitrary")),
        name="gla",
    )(proj, proj, proj, proj, glr, wg, bg, gn, tri)


def _rms(x, g):
    ms = jnp.mean(x * x, axis=-1, keepdims=True)
    return x * lax.rsqrt(ms + NORM_EPS) * g


def _post_kernel(x_ref, o1_ref, o2_ref, o3_ref, s1_ref, s2_ref, s3_ref, gla_ref, ga_ref, gb_ref,
                 wa_ref, wb_ref, wo_ref, g2_ref, wi_ref, wd_ref, gf_ref, out_ref, *, ffn_chunk):
    o_refs = (o1_ref, o2_ref, o3_ref)
    stats = [s[...] for s in (s1_ref, s2_ref, s3_ref)]
    heads = []
    for h in range(HEADS_PER_GROUP):
        hs = slice(h * HEAD_DIM, (h + 1) * HEAD_DIM)
        ms = [s[:, h:h + 1] for s in stats]
        ls = [s[:, HEADS_PER_GROUP + h:HEADS_PER_GROUP + h + 1] for s in stats]
        m_all = jnp.maximum(jnp.maximum(ms[0], ms[1]), ms[2])
        ws = [l * jnp.exp(m - m_all) for m, l in zip(ms, ls)]
        inv = 1.0 / (ws[0] + ws[1] + ws[2])
        acc = (ws[0] * inv) * o_refs[0][:, hs].astype(F32)
        acc = acc + (ws[1] * inv) * o_refs[1][:, hs].astype(F32)
        acc = acc + (ws[2] * inv) * o_refs[2][:, hs].astype(F32)
        heads.append(acc.astype(BF16))
    attn = jnp.concatenate(heads, axis=1)

    ya = jnp.dot(attn, wa_ref[...], preferred_element_type=F32)
    yb = jnp.dot(gla_ref[...], wb_ref[...], preferred_element_type=F32)
    mix = (jax.nn.sigmoid(ga_ref[...].astype(F32)) * ya
           + jax.nn.sigmoid(gb_ref[...].astype(F32)) * yb)
    x1 = x_ref[...] + jnp.dot(mix.astype(BF16), wo_ref[...], preferred_element_type=F32)

    h2 = _rms(x1, g2_ref[...]).astype(BF16)
    acc = x1
    for c in range(FFN_HIDDEN // ffn_chunk):
        gcols = slice(c * ffn_chunk, (c + 1) * ffn_chunk)
        ucols = slice(FFN_HIDDEN + c * ffn_chunk, FFN_HIDDEN + (c + 1) * ffn_chunk)
        g = jnp.dot(h2, wi_ref[:, gcols], preferred_element_type=F32)
        u = jnp.dot(h2, wi_ref[:, ucols], preferred_element_type=F32)
        a = (g * jax.nn.sigmoid(g) * u).astype(BF16)
        acc = acc + jnp.dot(a, wd_ref[gcols, :], preferred_element_type=F32)
    out_ref[...] = _rms(acc, gf_ref[...])


def _post(x2d, attn_o, attn_st, gla_out, proj, wa, wb, wo, g2, wi, wd, gf, tm, ffn_chunk):
    t = x2d.shape[0]
    resident = lambda shape: pl.BlockSpec(shape, lambda i: (0, 0), pipeline_mode=pl.Buffered(1))
    rows = lambda width, col=0: pl.BlockSpec((tm, width), lambda i: (i, col))
    return pl.pallas_call(
        functools.partial(_post_kernel, ffn_chunk=ffn_chunk),
        grid=(t // tm,),
        in_specs=[
            rows(D_MODEL),
            rows(GROUP_WIDTH), rows(GROUP_WIDTH), rows(GROUP_WIDTH),
            rows(LANES), rows(LANES), rows(LANES),
            rows(D_MODEL),
            rows(D_MODEL, CT_GA // 2), rows(D_MODEL, CT_GB // 2),
            resident(wa.shape), resident(wb.shape), resident(wo.shape), resident(g2.shape),
            resident(wi.shape), resident(wd.shape), resident(gf.shape),
        ],
        out_specs=rows(D_MODEL),
        out_shape=jax.ShapeDtypeStruct((t, D_MODEL), F32),
        compiler_params=_params(("parallel",)),
        name="post",
    )(x2d, *attn_o, *attn_st, gla_out, proj, proj, wa, wb, wo, g2, wi, wd, gf)


def _rope_tables(seq):
    half = ROPE_DIM // 2
    inv_freq = ROPE_THETA ** (-jnp.arange(0, ROPE_DIM, 2, dtype=F32) / ROPE_DIM)
    ang = jnp.arange(seq, dtype=F32)[:, None] * inv_freq[None, :]
    ang = jnp.concatenate([ang, ang], axis=-1)
    cos, sin = jnp.cos(ang), jnp.sin(ang)
    cos_t = jnp.concatenate([cos, jnp.ones((seq, HEAD_DIM - ROPE_DIM), F32)], axis=-1)
    sina_t = jnp.concatenate([-sin[:, :half], jnp.zeros((seq, HEAD_DIM - half), F32)], axis=-1)
    sinb_t = jnp.concatenate([jnp.zeros((seq, half), F32), sin[:, half:],
                              jnp.zeros((seq, HEAD_DIM - ROPE_DIM), F32)], axis=-1)
    return cos_t, sina_t, sinb_t


def _block_tril(n, blk):
    r = np.arange(n)
    return jnp.asarray((r[:, None] // blk == r[None, :] // blk) & (r[None, :] <= r[:, None]), BF16)


def _layer(x2d, batch, seq, norm1_g, w_in, w_gate_lr, b_gate, gla_norm_g, w_branch_a, w_branch_b,
           w_out, norm2_g, w_ffn_in, w_ffn_down, out_g, tables):
    a_w = 3 * GROUP_WIDTH
    gk_w = GLA_HEADS * GLA_KEY_DIM
    gv_w = GLA_HEADS * GLA_VAL_DIM
    o_aq, o_ak, o_av = 0, a_w, 2 * a_w
    o_gq = 3 * a_w
    o_gk = o_gq + gk_w
    o_gv = o_gk + gk_w
    o_gr = o_gv + gv_w
    o_glr = o_gr + gv_w
    o_ga = o_glr + GLA_GATE_RANK
    o_gb = o_ga + D_MODEL
    cols = lambda o, w: w_in[:, o:o + w]
    qkv = lambda g: [cols(o + g * GROUP_WIDTH, GROUP_WIDTH) for o in (o_aq, o_ak, o_av)]
    w_main = jnp.concatenate(
        [cols(o_gv, gv_w), cols(o_gr, gv_w), cols(o_ga, D_MODEL), cols(o_gb, D_MODEL)]
        + qkv(0) + [cols(o_gq, gk_w), cols(o_gk, gk_w)] + qkv(1) + qkv(2), axis=1).astype(BF16)
    w_glr = jnp.pad(cols(o_glr, GLA_GATE_RANK), ((0, 0), (0, LANES - GLA_GATE_RANK))).astype(BF16)
    wg = jnp.pad(w_gate_lr, ((0, LANES - GLA_GATE_RANK), (0, 0))).astype(BF16)

    proj, glr, qkv2, qkv3 = _inproj(x2d, norm1_g[None, :], w_main, w_glr, *tables,
                                    batch=batch, seq=seq, tm=1024)

    qkv1 = proj.reshape(batch, 1, seq, MAIN_WIDTH)
    attn = [
        _attention_group(qkv1, (CT_Q1, CT_K1, CT_V1), 1, batch, seq, tile_positions=512),
        _attention_group(qkv2, (0, 1, 2), 4, batch, seq, tile_positions=2048),
        _attention_group(qkv3, (0, 1, 2), 16, batch, seq, tile_positions=2048),
    ]
    attn_o = [o for o, _ in attn]
    attn_st = [st for _, st in attn]

    tc = 256
    gla_out = _gla(proj, glr, wg, b_gate[None, :], gla_norm_g[None, :], _block_tril(tc, 128),
                   batch, seq, tc=tc, chunk=128)

    return _post(x2d, attn_o, attn_st, gla_out, proj,
                 w_branch_a.astype(BF16), w_branch_b.astype(BF16), w_out.astype(BF16),
                 norm2_g[None, :], w_ffn_in.astype(BF16), w_ffn_down.astype(BF16), out_g[None, :],
                 tm=256, ffn_chunk=1408)


def kernel(x, norm1_g, w_in, w_gate_lr, b_gate, gla_norm_g, w_branch_a, w_branch_b, w_out, norm2_g,
           w_ffn_in, w_ffn_down, norm_f_g):
    batch, seq, d = x.shape
    depth = w_in.shape[0]
    assert depth == 1 and d == D_MODEL
    tables = _rope_tables(seq)
    x2d = x.reshape(batch * seq, d)
    out = _layer(x2d, batch, seq, norm1_g[0], w_in[0], w_gate_lr[0], b_gate[0], gla_norm_g[0],
                 w_branch_a[0], w_branch_b[0], w_out[0], norm2_g[0], w_ffn_in[0], w_ffn_down[0],
                 norm_f_g, tables)
    return out.reshape(batch, seq, d)
```

```python
import functools

import jax
import jax.numpy as jnp
import numpy as np
from jax import lax
from jax.experimental import pallas as pl
from jax.experimental.pallas import tpu as pltpu

F32 = jnp.float32
BF16 = jnp.bfloat16

D_MODEL = 1024
ATTN_GROUPS = ((128, 1), (512, 4), (2048, 16))
HEADS_PER_GROUP = 4
HEAD_DIM = 128
GROUP_WIDTH = HEADS_PER_GROUP * HEAD_DIM
KEYS_BACK = 128
ROPE_THETA = 500000.0
ROPE_DIM = HEAD_DIM // 4
GLA_HEADS = 4
GLA_KEY_DIM = 128
GLA_VAL_DIM = 256
GLA_GATE_RANK = 16
GLA_GATE_NORMALIZER = 16.0
GLA_SUB = 64
FFN_HIDDEN = 2816
NORM_EPS = 1e-6

LANES = 128
VMEM_LIMIT_BYTES = 56 * 1024 * 1024

COL_TILE = 512
CT_GV, CT_GR, CT_GA, CT_GB = 0, 2, 4, 6
CT_Q1, CT_K1, CT_V1 = 8, 9, 10
CT_GQ, CT_GK = 11, 12
N_MAIN_TILES = 13
MAIN_WIDTH = N_MAIN_TILES * COL_TILE
CT_G2, CT_G3 = 13, 16
N_COL_TILES = 19
QKV_WIDTH = 3 * COL_TILE


def _params(semantics):
    return pltpu.CompilerParams(dimension_semantics=semantics, vmem_limit_bytes=VMEM_LIMIT_BYTES)


def _inproj_kernel(x_ref, g_ref, w_ref, wglr_ref, cos_ref, sina_ref, sinb_ref,
                   out_ref, glr_ref, d2_ref, d3_ref, h_sc, slab_sc, *, tm):
    x = x_ref[...]
    ms = jnp.mean(x * x, axis=-1, keepdims=True)
    h_sc[...] = (x * lax.rsqrt(ms + NORM_EPS) * g_ref[...]).astype(BF16)
    glr_ref[...] = jnp.dot(h_sc[...], wglr_ref[...], preferred_element_type=F32).astype(BF16)

    plans = ((0, CT_Q1, 1, out_ref, ()), (CT_Q1, 3, 1, out_ref, (0, 1)),
             (CT_GQ, 2, 1, out_ref, ()), (CT_G2, 3, 4, d2_ref, (0, 1)),
             (CT_G3, 3, 16, d3_ref, (0, 1)))
    slab = 0
    for first, count, dilation, dst_ref, rope_tiles in plans:
        for jt in range(count):
            j = first + jt
            acc = jnp.dot(h_sc[...], w_ref[:, j * COL_TILE:(j + 1) * COL_TILE],
                          preferred_element_type=F32)
            for hh in range(COL_TILE // HEAD_DIM):
                a = acc[:, hh * HEAD_DIM:(hh + 1) * HEAD_DIM]
                if jt in rope_tiles:
                    up = pltpu.roll(a, HEAD_DIM - ROPE_DIM // 2, 1)
                    dn = pltpu.roll(a, ROPE_DIM // 2, 1)
                    a = a * cos_ref[...] + up * sina_ref[...] + dn * sinb_ref[...]
                if dilation == 1:
                    lo = j * COL_TILE + hh * HEAD_DIM
                    dst_ref[:, lo:lo + HEAD_DIM] = a.astype(BF16)
                else:
                    lo = jt * COL_TILE + hh * HEAD_DIM
                    slab_sc[slab] = a
                    for c in range(dilation):
                        run = slab_sc[slab, pl.ds(c, tm // dilation, stride=dilation), :]
                        dst_ref[0, c, :, lo:lo + HEAD_DIM] = run.astype(BF16)
                    slab += 1


N_DILATED_SLABS = 2 * 3 * (COL_TILE // HEAD_DIM)


def _inproj(x2d, g1, w_main, w_glr, cos_t, sina_t, sinb_t, batch, seq, tm):
    t = x2d.shape[0]
    pos_blocks = seq // tm
    table_spec = pl.BlockSpec((tm, LANES), lambda i: (i % pos_blocks, 0))
    resident = lambda shape: pl.BlockSpec(shape, lambda i: (0, 0), pipeline_mode=pl.Buffered(1))

    def dilated_spec(dilation):
        return pl.BlockSpec((1, dilation, tm // dilation, QKV_WIDTH),
                            lambda i: (i // pos_blocks, 0, i % pos_blocks, 0))

    return pl.pallas_call(
        functools.partial(_inproj_kernel, tm=tm),
        grid=(t // tm,),
        in_specs=[
            pl.BlockSpec((tm, D_MODEL), lambda i: (i, 0)),
            resident((1, D_MODEL)),
            resident(w_main.shape),
            resident(w_glr.shape),
            table_spec, table_spec, table_spec,
        ],
        out_specs=[
            pl.BlockSpec((tm, MAIN_WIDTH), lambda i: (i, 0)),
            pl.BlockSpec((tm, LANES), lambda i: (i, 0)),
            dilated_spec(4),
            dilated_spec(16),
        ],
        out_shape=[
            jax.ShapeDtypeStruct((t, MAIN_WIDTH), BF16),
            jax.ShapeDtypeStruct((t, LANES), BF16),
            jax.ShapeDtypeStruct((batch, 4, seq // 4, QKV_WIDTH), BF16),
            jax.ShapeDtypeStruct((batch, 16, seq // 16, QKV_WIDTH), BF16),
        ],
        scratch_shapes=[pltpu.VMEM((tm, D_MODEL), BF16),
                        pltpu.VMEM((N_DILATED_SLABS, tm, HEAD_DIM), F32)],
        compiler_params=_params(("parallel",)),
        name="inproj",
    )(x2d, g1, w_main, w_glr, cos_t, sina_t, sinb_t)


def _attn_kernel(q_ref, kp_ref, kc_ref, vp_ref, vc_ref, o_ref, st_ref, *scratch, tq, dilation):
    n = pl.program_id(1)
    blk = KEYS_BACK
    qi = lax.broadcasted_iota(jnp.int32, (blk, 2 * blk), 0)
    jj = lax.broadcasted_iota(jnp.int32, (blk, 2 * blk), 1)
    band = jnp.logical_and(jj >= qi, jj <= qi + blk)
    first_band = jnp.logical_and(band, jj + n * tq >= blk)
    lane = lax.broadcasted_iota(jnp.int32, (blk, LANES), 1)
    scale = HEAD_DIM ** -0.5
    neg_inf = jnp.float32(-jnp.inf)

    def phase(c):
        for qb in range(tq // blk):
            rows = slice(qb * blk, (qb + 1) * blk)
            valid = first_band if qb == 0 else band
            stats = jnp.zeros((blk, LANES), F32)
            nat = pl.ds(c + qb * blk * dilation, blk, stride=dilation)
            for h in range(HEADS_PER_GROUP):
                hs = slice(h * HEAD_DIM, (h + 1) * HEAD_DIM)
                q = q_ref[0, c, rows, hs]
                if qb == 0:
                    kk = jnp.concatenate([kp_ref[0, c, :, hs], kc_ref[0, c, 0:blk, hs]], axis=0)
                    vv = jnp.concatenate([vp_ref[0, c, :, hs], vc_ref[0, c, 0:blk, hs]], axis=0)
                else:
                    kk = kc_ref[0, c, (qb - 1) * blk:(qb + 1) * blk, hs]
                    vv = vc_ref[0, c, (qb - 1) * blk:(qb + 1) * blk, hs]
                s = lax.dot_general(q, kk, (((1,), (1,)), ((), ())), preferred_element_type=F32)
                s = jnp.where(valid, s * scale, neg_inf)
                m = jnp.max(s, axis=-1, keepdims=True)
                p = jnp.exp(s - m)
                l = jnp.sum(p, axis=-1, keepdims=True)
                o = jnp.dot(p.astype(BF16), vv, preferred_element_type=F32) / l
                if dilation == 1:
                    o_ref[0, rows, hs] = o.astype(BF16)
                else:
                    scratch[0][h, nat, :] = o
                stats = jnp.where(lane == h, m, stats)
                stats = jnp.where(lane == HEADS_PER_GROUP + h, l, stats)
            if dilation == 1:
                st_ref[0, rows, :] = stats
            else:
                st_ref[0, nat, :] = stats

    if dilation == 1:
        phase(0)
    else:
        def body(c, carry):
            phase(c)
            return carry
        lax.fori_loop(0, dilation, body, 0)
        for h in range(HEADS_PER_GROUP):
            o_ref[0, :, h * HEAD_DIM:(h + 1) * HEAD_DIM] = scratch[0][h].astype(BF16)


def _attention_group(qkv, col_tiles, dilation, batch, seq, tile_positions):
    r = dilation
    tq = tile_positions // r
    nblk = seq // tile_positions
    prev_per_blk = tq // KEYS_BACK
    cq, ck, cv = col_tiles

    def cur(ct):
        return pl.BlockSpec((1, r, tq, GROUP_WIDTH), lambda b, n: (b, 0, n, ct))

    def prev(ct):
        return pl.BlockSpec((1, r, KEYS_BACK, GROUP_WIDTH),
                            lambda b, n: (b, 0, jnp.maximum(n * prev_per_blk - 1, 0), ct))

    scratch = []
    if r > 1:
        scratch.append(pltpu.VMEM((HEADS_PER_GROUP, tile_positions, HEAD_DIM), F32))
    o, st = pl.pallas_call(
        functools.partial(_attn_kernel, tq=tq, dilation=r),
        grid=(batch, nblk),
        in_specs=[cur(cq), prev(ck), cur(ck), prev(cv), cur(cv)],
        out_specs=[
            pl.BlockSpec((1, tile_positions, GROUP_WIDTH), lambda b, n: (b, n, 0)),
            pl.BlockSpec((1, tile_positions, LANES), lambda b, n: (b, n, 0)),
        ],
        out_shape=[
            jax.ShapeDtypeStruct((batch, seq, GROUP_WIDTH), BF16),
            jax.ShapeDtypeStruct((batch, seq, LANES), F32),
        ],
        scratch_shapes=scratch,
        compiler_params=_params(("parallel", "arbitrary")),
        name=f"attn_r{r}",
    )(qkv, qkv, qkv, qkv, qkv)
    return o.reshape(batch * seq, GROUP_WIDTH), st.reshape(batch * seq, LANES)


def _gla_kernel(q_ref, k_ref, v_ref, gr_ref, glr_ref, wg_ref, bg_ref, gn_ref, tri_ref,
                o_ref, state_sc, *, tc, chunk):
    n = pl.program_id(1)

    @pl.when(n == 0)
    def _():
        state_sc[...] = jnp.zeros_like(state_sc)

    z = jnp.dot(glr_ref[...], wg_ref[...], preferred_element_type=F32) + bg_ref[...]
    log_a = (jnp.minimum(z, 0.0) - jnp.log(1.0 + jnp.exp(-jnp.abs(z)))) / GLA_GATE_NORMALIZER
    hi = log_a.astype(BF16)
    lo = (log_a - hi.astype(F32)).astype(BF16)
    tri = tri_ref[...]
    bcum = (jnp.dot(tri, hi, preferred_element_type=F32)
            + jnp.dot(tri, lo, preferred_element_type=F32))

    sub = GLA_SUB
    nsub = chunk // sub
    sub_shift = sub.bit_length() - 1
    row = lax.broadcasted_iota(jnp.int32, (chunk, chunk), 0)
    colm = lax.broadcasted_iota(jnp.int32, (chunk, chunk), 1)
    diag_mask = jnp.logical_and((row >> sub_shift) == (colm >> sub_shift), colm <= row)
    rsub = lax.broadcasted_iota(jnp.int32, (chunk, GLA_KEY_DIM), 0) >> sub_shift
    q_scale = GLA_KEY_DIM ** -0.5
    dn_t = (((1,), (1,)), ((), ()))
    dn_l = (((0,), (0,)), ((), ()))

    for h in range(GLA_HEADS):
        ks = slice(h * GLA_KEY_DIM, (h + 1) * GLA_KEY_DIM)
        vs = slice(h * GLA_VAL_DIM, (h + 1) * GLA_VAL_DIM)
        for c in range(tc // chunk):
            rows = slice(c * chunk, (c + 1) * chunk)
            b = bcum[rows, ks]
            q = q_ref[rows, ks].astype(F32) * q_scale
            k = k_ref[rows, ks].astype(F32)
            v = v_ref[rows, vs]
            centre = b[sub // 2:sub // 2 + 1, :]
            for sb in range(1, nsub):
                centre = jnp.where(rsub == sb, b[sb * sub + sub // 2:sb * sub + sub // 2 + 1, :], centre)
            qc = (q * jnp.exp(b - centre)).astype(BF16)
            kc = (k * jnp.exp(centre - b)).astype(BF16)
            a = jnp.where(diag_mask, lax.dot_general(qc, kc, dn_t, preferred_element_type=F32), 0.0)
            for sb in range(1, nsub):
                lo_r, hi_r = sb * sub, (sb + 1) * sub
                bound = b[lo_r - 1:lo_r, :]
                qb = (q[lo_r:hi_r] * jnp.exp(b[lo_r:hi_r] - bound)).astype(BF16)
                kb = (k[0:lo_r] * jnp.exp(bound - b[0:lo_r])).astype(BF16)
                kb = jnp.concatenate([kb, jnp.zeros((chunk - lo_r, GLA_KEY_DIM), BF16)], axis=0)
                off = lax.dot_general(qb, kb, dn_t, preferred_element_type=F32)
                pieces = []
                if lo_r:
                    pieces.append(jnp.zeros((lo_r, chunk), F32))
                pieces.append(off)
                if chunk - hi_r:
                    pieces.append(jnp.zeros((chunk - hi_r, chunk), F32))
                a = a + jnp.concatenate(pieces, axis=0)
            b_last = b[chunk - 1:chunk, :]
            q_in = (q * jnp.exp(b)).astype(BF16)
            k_st = (k * jnp.exp(b_last - b)).astype(BF16)
            st = state_sc[h]
            o = (jnp.dot(a.astype(BF16), v, preferred_element_type=F32)
                 + lax.dot_general(q_in, st.astype(BF16), dn_t, preferred_element_type=F32))
            state_sc[h] = jnp.exp(b_last) * st + lax.dot_general(v, k_st, dn_l,
                                                                   preferred_element_type=F32)
            ms = jnp.mean(o * o, axis=-1, keepdims=True)
            y = o * lax.rsqrt(ms + NORM_EPS) * gn_ref[...]
            g = gr_ref[rows, vs].astype(F32)
            o_ref[rows, vs] = (y * (g * jax.nn.sigmoid(g))).astype(BF16)


def _gla(proj, glr, wg, bg, gn, tri, batch, seq, tc, chunk):
    nblk = seq // tc
    row = lambda b, n: b * nblk + n
    return pl.pallas_call(
        functools.partial(_gla_kernel, tc=tc, chunk=chunk),
        grid=(batch, nblk),
        in_specs=[
            pl.BlockSpec((tc, COL_TILE), lambda b, n: (row(b, n), CT_GQ)),
            pl.BlockSpec((tc, COL_TILE), lambda b, n: (row(b, n), CT_GK)),
            pl.BlockSpec((tc, 2 * COL_TILE), lambda b, n: (row(b, n), CT_GV // 2)),
            pl.BlockSpec((tc, 2 * COL_TILE), lambda b, n: (row(b, n), CT_GR // 2)),
            pl.BlockSpec((tc, LANES), lambda b, n: (row(b, n), 0)),
            pl.BlockSpec((LANES, GLA_HEADS * GLA_KEY_DIM), lambda b, n: (0, 0)),
            pl.BlockSpec((1, GLA_HEADS * GLA_KEY_DIM), lambda b, n: (0, 0)),
            pl.BlockSpec((1, GLA_VAL_DIM), lambda b, n: (0, 0)),
            pl.BlockSpec((tc, tc), lambda b, n: (0, 0)),
        ],
        out_specs=pl.BlockSpec((tc, GLA_HEADS * GLA_VAL_DIM), lambda b, n: (row(b, n), 0)),
        out_shape=jax.ShapeDtypeStruct((batch * seq, GLA_HEADS * GLA_VAL_DIM), BF16),
        scratch_shapes=[pltpu.VMEM((GLA_HEADS, GLA_VAL_DIM, GLA_KEY_DIM), F32)],
        compiler_params=_params(("parallel", "arbitrary")),
        name="gla",
    )(proj, proj, proj, proj, glr, wg, bg, gn, tri)


def _rms(x, g):
    ms = jnp.mean(x * x, axis=-1, keepdims=True)
    return x * lax.rsqrt(ms + NORM_EPS) * g


def _post_kernel(x_ref, o1_ref, o2_ref, o3_ref, s1_ref, s2_ref, s3_ref, gla_ref, ga_ref, gb_ref,
                 wa_ref, wb_ref, wo_ref, g2_ref, wi_ref, wd_ref, gf_ref, out_ref, *, ffn_chunk):
    o_refs = (o1_ref, o2_ref, o3_ref)
    stats = [s[...] for s in (s1_ref, s2_ref, s3_ref)]
    heads = []
    for h in range(HEADS_PER_GROUP):
        hs = slice(h * HEAD_DIM, (h + 1) * HEAD_DIM)
        ms = [s[:, h:h + 1] for s in stats]
        ls = [s[:, HEADS_PER_GROUP + h:HEADS_PER_GROUP + h + 1] for s in stats]
        m_all = jnp.maximum(jnp.maximum(ms[0], ms[1]), ms[2])
        ws = [l * jnp.exp(m - m_all) for m, l in zip(ms, ls)]
        inv = 1.0 / (ws[0] + ws[1] + ws[2])
        acc = (ws[0] * inv) * o_refs[0][:, hs].astype(F32)
        acc = acc + (ws[1] * inv) * o_refs[1][:, hs].astype(F32)
        acc = acc + (ws[2] * inv) * o_refs[2][:, hs].astype(F32)
        heads.append(acc.astype(BF16))
    attn = jnp.concatenate(heads, axis=1)

    ya = jnp.dot(attn, wa_ref[...], preferred_element_type=F32)
    yb = jnp.dot(gla_ref[...], wb_ref[...], preferred_element_type=F32)
    mix = (jax.nn.sigmoid(ga_ref[...].astype(F32)) * ya
           + jax.nn.sigmoid(gb_ref[...].astype(F32)) * yb)
    x1 = x_ref[...] + jnp.dot(mix.astype(BF16), wo_ref[...], preferred_element_type=F32)

    h2 = _rms(x1, g2_ref[...]).astype(BF16)
    acc = x1
    for c in range(FFN_HIDDEN // ffn_chunk):
        gcols = slice(c * ffn_chunk, (c + 1) * ffn_chunk)
        ucols = slice(FFN_HIDDEN + c * ffn_chunk, FFN_HIDDEN + (c + 1) * ffn_chunk)
        g = jnp.dot(h2, wi_ref[:, gcols], preferred_element_type=F32)
        u = jnp.dot(h2, wi_ref[:, ucols], preferred_element_type=F32)
        a = (g * jax.nn.sigmoid(g) * u).astype(BF16)
        acc = acc + jnp.dot(a, wd_ref[gcols, :], preferred_element_type=F32)
    out_ref[...] = _rms(acc, gf_ref[...])


def _post(x2d, attn_o, attn_st, gla_out, proj, wa, wb, wo, g2, wi, wd, gf, tm, ffn_chunk):
    t = x2d.shape[0]
    resident = lambda shape: pl.BlockSpec(shape, lambda i: (0, 0), pipeline_mode=pl.Buffered(1))
    rows = lambda width, col=0: pl.BlockSpec((tm, width), lambda i: (i, col))
    return pl.pallas_call(
        functools.partial(_post_kernel, ffn_chunk=ffn_chunk),
        grid=(t // tm,),
        in_specs=[
            rows(D_MODEL),
            rows(GROUP_WIDTH), rows(GROUP_WIDTH), rows(GROUP_WIDTH),
            rows(LANES), rows(LANES), rows(LANES),
            rows(D_MODEL),
            rows(D_MODEL, CT_GA // 2), rows(D_MODEL, CT_GB // 2),
            resident(wa.shape), resident(wb.shape), resident(wo.shape), resident(g2.shape),
            resident(wi.shape), resident(wd.shape), resident(gf.shape),
        ],
        out_specs=rows(D_MODEL),
        out_shape=jax.ShapeDtypeStruct((t, D_MODEL), F32),
        compiler_params=_params(("parallel",)),
        name="post",
    )(x2d, *attn_o, *attn_st, gla_out, proj, proj, wa, wb, wo, g2, wi, wd, gf)


def _rope_tables(seq):
    half = ROPE_DIM // 2
    inv_freq = ROPE_THETA ** (-jnp.arange(0, ROPE_DIM, 2, dtype=F32) / ROPE_DIM)
    ang = jnp.arange(seq, dtype=F32)[:, None] * inv_freq[None, :]
    ang = jnp.concatenate([ang, ang], axis=-1)
    cos, sin = jnp.cos(ang), jnp.sin(ang)
    cos_t = jnp.concatenate([cos, jnp.ones((seq, HEAD_DIM - ROPE_DIM), F32)], axis=-1)
    sina_t = jnp.concatenate([-sin[:, :half], jnp.zeros((seq, HEAD_DIM - half), F32)], axis=-1)
    sinb_t = jnp.concatenate([jnp.zeros((seq, half), F32), sin[:, half:],
                              jnp.zeros((seq, HEAD_DIM - ROPE_DIM), F32)], axis=-1)
    return cos_t, sina_t, sinb_t


def _block_tril(n, blk):
    r = np.arange(n)
    return jnp.asarray((r[:, None] // blk == r[None, :] // blk) & (r[None, :] <= r[:, None]), BF16)


def _layer(x2d, batch, seq, norm1_g, w_in, w_gate_lr, b_gate, gla_norm_g, w_branch_a, w_branch_b,
           w_out, norm2_g, w_ffn_in, w_ffn_down, out_g, tables):
    a_w = 3 * GROUP_WIDTH
    gk_w = GLA_HEADS * GLA_KEY_DIM
    gv_w = GLA_HEADS * GLA_VAL_DIM
    o_aq, o_ak, o_av = 0, a_w, 2 * a_w
    o_gq = 3 * a_w
    o_gk = o_gq + gk_w
    o_gv = o_gk + gk_w
    o_gr = o_gv + gv_w
    o_glr = o_gr + gv_w
    o_ga = o_glr + GLA_GATE_RANK
    o_gb = o_ga + D_MODEL
    cols = lambda o, w: w_in[:, o:o + w]
    qkv = lambda g: [cols(o + g * GROUP_WIDTH, GROUP_WIDTH) for o in (o_aq, o_ak, o_av)]
    w_main = jnp.concatenate(
        [cols(o_gv, gv_w), cols(o_gr, gv_w), cols(o_ga, D_MODEL), cols(o_gb, D_MODEL)]
        + qkv(0) + [cols(o_gq, gk_w), cols(o_gk, gk_w)] + qkv(1) + qkv(2), axis=1).astype(BF16)
    w_glr = jnp.pad(cols(o_glr, GLA_GATE_RANK), ((0, 0), (0, LANES - GLA_GATE_RANK))).astype(BF16)
    wg = jnp.pad(w_gate_lr, ((0, LANES - GLA_GATE_RANK), (0, 0))).astype(BF16)

    proj, glr, qkv2, qkv3 = _inproj(x2d, norm1_g[None, :], w_main, w_glr, *tables,
                                    batch=batch, seq=seq, tm=256)

    qkv1 = proj.reshape(batch, 1, seq, MAIN_WIDTH)
    attn = [
        _attention_group(qkv1, (CT_Q1, CT_K1, CT_V1), 1, batch, seq, tile_positions=512),
        _attention_group(qkv2, (0, 1, 2), 4, batch, seq, tile_positions=2048),
        _attention_group(qkv3, (0, 1, 2), 16, batch, seq, tile_positions=2048),
    ]
    attn_o = [o for o, _ in attn]
    attn_st = [st for _, st in attn]

    tc = 256
    gla_out = _gla(proj, glr, wg, b_gate[None, :], gla_norm_g[None, :], _block_tril(tc, 128),
                   batch, seq, tc=tc, chunk=128)

    return _post(x2d, attn_o, attn_st, gla_out, proj,
                 w_branch_a.astype(BF16), w_branch_b.astype(BF16), w_out.astype(BF16),
                 norm2_g[None, :], w_ffn_in.astype(BF16), w_ffn_down.astype(BF16), out_g[None, :],
                 tm=512, ffn_chunk=1408)


def kernel(x, norm1_g, w_in, w_gate_lr, b_gate, gla_norm_g, w_branch_a, w_branch_b, w_out, norm2_g,
           w_ffn_in, w_ffn_down, norm_f_g):
    batch, seq, d = x.shape
    depth = w_in.shape[0]
    assert depth == 1 and d == D_MODEL
    tables = _rope_tables(seq)
    x2d = x.reshape(batch * seq, d)
    out = _layer(x2d, batch, seq, norm1_g[0], w_in[0], w_gate_lr[0], b_gate[0], gla_norm_g[0],
                 w_branch_a[0], w_branch_b[0], w_out[0], norm2_g[0], w_ffn_in[0], w_ffn_down[0],
                 norm_f_g, tables)
    return out.reshape(batch, seq, d)
```

```python
import functools

import jax
import jax.numpy as jnp
import numpy as np
from jax import lax
from jax.experimental import pallas as pl
from jax.experimental.pallas import tpu as pltpu

F32 = jnp.float32
BF16 = jnp.bfloat16

D_MODEL = 1024
ATTN_GROUPS = ((128, 1), (512, 4), (2048, 16))
HEADS_PER_GROUP = 4
HEAD_DIM = 128
GROUP_WIDTH = HEADS_PER_GROUP * HEAD_DIM
KEYS_BACK = 128
BLOCK_HEADS_PER_TRIP = 16
ROPE_THETA = 500000.0
ROPE_DIM = HEAD_DIM // 4
GLA_HEADS = 4
GLA_KEY_DIM = 128
GLA_VAL_DIM = 256
GLA_GATE_RANK = 16
GLA_GATE_NORMALIZER = 16.0
GLA_SUB = 64
FFN_HIDDEN = 2816
NORM_EPS = 1e-6

LANES = 128
VMEM_LIMIT_BYTES = 56 * 1024 * 1024

COL_TILE = 512
CT_GV, CT_GR, CT_GA, CT_GB = 0, 2, 4, 6
CT_Q1, CT_K1, CT_V1 = 8, 9, 10
CT_GQ, CT_GK = 11, 12
N_MAIN_TILES = 13
MAIN_WIDTH = N_MAIN_TILES * COL_TILE
CT_G2, CT_G3 = 13, 16
N_COL_TILES = 19
QKV_WIDTH = 3 * COL_TILE


def _params(semantics):
    return pltpu.CompilerParams(dimension_semantics=semantics, vmem_limit_bytes=VMEM_LIMIT_BYTES)


def _inproj_kernel(x_ref, g_ref, w_ref, wglr_ref, cos_ref, sina_ref, sinb_ref,
                   out_ref, glr_ref, d2_ref, d3_ref, h_sc, slab_sc, *, tm):
    x = x_ref[...]
    ms = jnp.mean(x * x, axis=-1, keepdims=True)
    h_sc[...] = (x * lax.rsqrt(ms + NORM_EPS) * g_ref[...]).astype(BF16)
    glr_ref[...] = jnp.dot(h_sc[...], wglr_ref[...], preferred_element_type=F32).astype(BF16)

    plans = ((CT_G3, 3, 16, d3_ref, (0, 1)), (CT_G2, 3, 4, d2_ref, (0, 1)),
             (CT_Q1, 3, 1, out_ref, (0, 1)), (CT_GQ, 2, 1, out_ref, ()),
             (0, CT_Q1, 1, out_ref, ()))
    slab = 0
    for first, count, dilation, dst_ref, rope_tiles in plans:
        for jt in range(count):
            j = first + jt
            acc = jnp.dot(h_sc[...], w_ref[:, j * COL_TILE:(j + 1) * COL_TILE],
                          preferred_element_type=F32)
            for hh in range(COL_TILE // HEAD_DIM):
                a = acc[:, hh * HEAD_DIM:(hh + 1) * HEAD_DIM]
                if jt in rope_tiles:
                    up = pltpu.roll(a, HEAD_DIM - ROPE_DIM // 2, 1)
                    dn = pltpu.roll(a, ROPE_DIM // 2, 1)
                    a = a * cos_ref[...] + up * sina_ref[...] + dn * sinb_ref[...]
                if dilation == 1:
                    lo = j * COL_TILE + hh * HEAD_DIM
                    dst_ref[:, lo:lo + HEAD_DIM] = a.astype(BF16)
                else:
                    lo = jt * COL_TILE + hh * HEAD_DIM
                    slab_sc[slab] = a
                    for c in range(dilation):
                        run = slab_sc[slab, pl.ds(c, tm // dilation, stride=dilation), :]
                        dst_ref[0, c, :, lo:lo + HEAD_DIM] = run.astype(BF16)
                    slab += 1


N_DILATED_SLABS = 2 * 3 * (COL_TILE // HEAD_DIM)


def _inproj(x2d, g1, w_main, w_glr, cos_t, sina_t, sinb_t, batch, seq, tm):
    t = x2d.shape[0]
    pos_blocks = seq // tm
    table_spec = pl.BlockSpec((tm, LANES), lambda i: (i % pos_blocks, 0))
    resident = lambda shape: pl.BlockSpec(shape, lambda i: (0, 0), pipeline_mode=pl.Buffered(1))

    def dilated_spec(dilation):
        return pl.BlockSpec((1, dilation, tm // dilation, QKV_WIDTH),
                            lambda i: (i // pos_blocks, 0, i % pos_blocks, 0))

    return pl.pallas_call(
        functools.partial(_inproj_kernel, tm=tm),
        grid=(t // tm,),
        in_specs=[
            pl.BlockSpec((tm, D_MODEL), lambda i: (i, 0)),
            resident((1, D_MODEL)),
            resident(w_main.shape),
            resident(w_glr.shape),
            table_spec, table_spec, table_spec,
        ],
        out_specs=[
            pl.BlockSpec((tm, MAIN_WIDTH), lambda i: (i, 0)),
            pl.BlockSpec((tm, LANES), lambda i: (i, 0)),
            dilated_spec(4),
            dilated_spec(16),
        ],
        out_shape=[
            jax.ShapeDtypeStruct((t, MAIN_WIDTH), BF16),
            jax.ShapeDtypeStruct((t, LANES), BF16),
            jax.ShapeDtypeStruct((batch, 4, seq // 4, QKV_WIDTH), BF16),
            jax.ShapeDtypeStruct((batch, 16, seq // 16, QKV_WIDTH), BF16),
        ],
        scratch_shapes=[pltpu.VMEM((tm, D_MODEL), BF16),
                        pltpu.VMEM((N_DILATED_SLABS, tm, HEAD_DIM), F32)],
        compiler_params=_params(("parallel",)),
        name="inproj",
    )(x2d, g1, w_main, w_glr, cos_t, sina_t, sinb_t)


def _attn_kernel(q_ref, kp_ref, kc_ref, vp_ref, vc_ref, o_ref, st_ref, *scratch, tq, dilation):
    n = pl.program_id(1)
    blk = KEYS_BACK
    qi = lax.broadcasted_iota(jnp.int32, (blk, 2 * blk), 0)
    jj = lax.broadcasted_iota(jnp.int32, (blk, 2 * blk), 1)
    band = jnp.logical_and(jj >= qi, jj <= qi + blk)
    first_band = jnp.logical_and(band, jj + n * tq >= blk)
    lane = lax.broadcasted_iota(jnp.int32, (blk, LANES), 1)
    scale = HEAD_DIM ** -0.5
    neg_inf = jnp.float32(-jnp.inf)

    def phase(c):
        for qb in range(tq // blk):
            rows = slice(qb * blk, (qb + 1) * blk)
            valid = first_band if qb == 0 else band
            stats = jnp.zeros((blk, LANES), F32)
            nat = pl.ds(c + qb * blk * dilation, blk, stride=dilation)
            for h in range(HEADS_PER_GROUP):
                hs = slice(h * HEAD_DIM, (h + 1) * HEAD_DIM)
                q = q_ref[0, c, rows, hs]
                if qb == 0:
                    kk = jnp.concatenate([kp_ref[0, c, :, hs], kc_ref[0, c, 0:blk, hs]], axis=0)
                    vv = jnp.concatenate([vp_ref[0, c, :, hs], vc_ref[0, c, 0:blk, hs]], axis=0)
                else:
                    kk = kc_ref[0, c, (qb - 1) * blk:(qb + 1) * blk, hs]
                    vv = vc_ref[0, c, (qb - 1) * blk:(qb + 1) * blk, hs]
                s = lax.dot_general(q, kk, (((1,), (1,)), ((), ())), preferred_element_type=F32)
                s = jnp.where(valid, s * scale, neg_inf)
                m = jnp.max(s, axis=-1, keepdims=True)
                p = jnp.exp(s - m)
                l = jnp.sum(p, axis=-1, keepdims=True)
                o = jnp.dot(p.astype(BF16), vv, preferred_element_type=F32) / l
                if dilation == 1:
                    o_ref[0, rows, hs] = o.astype(BF16)
                else:
                    scratch[0][h, nat, :] = o
                stats = jnp.where(lane == h, m, stats)
                stats = jnp.where(lane == HEADS_PER_GROUP + h, l, stats)
            if dilation == 1:
                st_ref[0, rows, :] = stats
            else:
                st_ref[0, nat, :] = stats

    if dilation == 1:
        phase(0)
    else:
        per_trip = max(1, BLOCK_HEADS_PER_TRIP // (HEADS_PER_GROUP * (tq // blk)))

        def body(t, carry):
            for u in range(per_trip):
                phase(t * per_trip + u)
            return carry
        lax.fori_loop(0, dilation // per_trip, body, 0)
        for h in range(HEADS_PER_GROUP):
            o_ref[0, :, h * HEAD_DIM:(h + 1) * HEAD_DIM] = scratch[0][h].astype(BF16)


def _attention_group(qkv, col_tiles, dilation, batch, seq, tile_positions):
    r = dilation
    tq = tile_positions // r
    nblk = seq // tile_positions
    prev_per_blk = tq // KEYS_BACK
    cq, ck, cv = col_tiles

    def cur(ct):
        return pl.BlockSpec((1, r, tq, GROUP_WIDTH), lambda b, n: (b, 0, n, ct))

    def prev(ct):
        return pl.BlockSpec((1, r, KEYS_BACK, GROUP_WIDTH),
                            lambda b, n: (b, 0, jnp.maximum(n * prev_per_blk - 1, 0), ct))

    scratch = []
    if r > 1:
        scratch.append(pltpu.VMEM((HEADS_PER_GROUP, tile_positions, HEAD_DIM), F32))
    o, st = pl.pallas_call(
        functools.partial(_attn_kernel, tq=tq, dilation=r),
        grid=(batch, nblk),
        in_specs=[cur(cq), prev(ck), cur(ck), prev(cv), cur(cv)],
        out_specs=[
            pl.BlockSpec((1, tile_positions, GROUP_WIDTH), lambda b, n: (b, n, 0)),
            pl.BlockSpec((1, tile_positions, LANES), lambda b, n: (b, n, 0)),
        ],
        out_shape=[
            jax.ShapeDtypeStruct((batch, seq, GROUP_WIDTH), BF16),
            jax.ShapeDtypeStruct((batch, seq, LANES), F32),
        ],
        scratch_shapes=scratch,
        compiler_params=_params(("parallel", "arbitrary")),
        name=f"attn_r{r}",
    )(qkv, qkv, qkv, qkv, qkv)
    return o.reshape(batch * seq, GROUP_WIDTH), st.reshape(batch * seq, LANES)


def _gla_kernel(q_ref, k_ref, v_ref, gr_ref, glr_ref, wg_ref, bg_ref, gn_ref, tri_ref,
                o_ref, state_sc, *, tc, chunk):
    n = pl.program_id(1)

    @pl.when(n == 0)
    def _():
        state_sc[...] = jnp.zeros_like(state_sc)

    z = jnp.dot(glr_ref[...], wg_ref[...], preferred_element_type=F32) + bg_ref[...]
    log_a = (jnp.minimum(z, 0.0) - jnp.log(1.0 + jnp.exp(-jnp.abs(z)))) / GLA_GATE_NORMALIZER
    hi = log_a.astype(BF16)
    lo = (log_a - hi.astype(F32)).astype(BF16)
    tri = tri_ref[...]
    bcum = (jnp.dot(tri, hi, preferred_element_type=F32)
            + jnp.dot(tri, lo, preferred_element_type=F32))

    sub = GLA_SUB
    nsub = chunk // sub
    sub_shift = sub.bit_length() - 1
    row = lax.broadcasted_iota(jnp.int32, (chunk, chunk), 0)
    colm = lax.broadcasted_iota(jnp.int32, (chunk, chunk), 1)
    diag_mask = jnp.logical_and((row >> sub_shift) == (colm >> sub_shift), colm <= row)
    rsub = lax.broadcasted_iota(jnp.int32, (chunk, GLA_KEY_DIM), 0) >> sub_shift
    q_scale = GLA_KEY_DIM ** -0.5
    dn_t = (((1,), (1,)), ((), ()))
    dn_l = (((0,), (0,)), ((), ()))

    for h in range(GLA_HEADS):
        ks = slice(h * GLA_KEY_DIM, (h + 1) * GLA_KEY_DIM)
        vs = slice(h * GLA_VAL_DIM, (h + 1) * GLA_VAL_DIM)
        for c in range(tc // chunk):
            rows = slice(c * chunk, (c + 1) * chunk)
            b = bcum[rows, ks]
            q = q_ref[rows, ks].astype(F32) * q_scale
            k = k_ref[rows, ks].astype(F32)
            v = v_ref[rows, vs]
            centre = b[sub // 2:sub // 2 + 1, :]
            for sb in range(1, nsub):
                centre = jnp.where(rsub == sb, b[sb * sub + sub // 2:sb * sub + sub // 2 + 1, :], centre)
            qc = (q * jnp.exp(b - centre)).astype(BF16)
            kc = (k * jnp.exp(centre - b)).astype(BF16)
            a = jnp.where(diag_mask, lax.dot_general(qc, kc, dn_t, preferred_element_type=F32), 0.0)
            for sb in range(1, nsub):
                lo_r, hi_r = sb * sub, (sb + 1) * sub
                bound = b[lo_r - 1:lo_r, :]
                qb = (q[lo_r:hi_r] * jnp.exp(b[lo_r:hi_r] - bound)).astype(BF16)
                kb = (k[0:lo_r] * jnp.exp(bound - b[0:lo_r])).astype(BF16)
                kb = jnp.concatenate([kb, jnp.zeros((chunk - lo_r, GLA_KEY_DIM), BF16)], axis=0)
                off = lax.dot_general(qb, kb, dn_t, preferred_element_type=F32)
                pieces = []
                if lo_r:
                    pieces.append(jnp.zeros((lo_r, chunk), F32))
                pieces.append(off)
                if chunk - hi_r:
                    pieces.append(jnp.zeros((chunk - hi_r, chunk), F32))
                a = a + jnp.concatenate(pieces, axis=0)
            b_last = b[chunk - 1:chunk, :]
            q_in = (q * jnp.exp(b)).astype(BF16)
            k_st = (k * jnp.exp(b_last - b)).astype(BF16)
            st = state_sc[h]
            o = (jnp.dot(a.astype(BF16), v, preferred_element_type=F32)
                 + lax.dot_general(q_in, st.astype(BF16), dn_t, preferred_element_type=F32))
            state_sc[h] = jnp.exp(b_last) * st + lax.dot_general(v, k_st, dn_l,
                                                                   preferred_element_type=F32)
            ms = jnp.mean(o * o, axis=-1, keepdims=True)
            y = o * lax.rsqrt(ms + NORM_EPS) * gn_ref[...]
            g = gr_ref[rows, vs].astype(F32)
            o_ref[rows, vs] = (y * (g * jax.nn.sigmoid(g))).astype(BF16)


def _gla(proj, glr, wg, bg, gn, tri, batch, seq, tc, chunk):
    nblk = seq // tc
    row = lambda b, n: b * nblk + n
    return pl.pallas_call(
        functools.partial(_gla_kernel, tc=tc, chunk=chunk),
        grid=(batch, nblk),
        in_specs=[
            pl.BlockSpec((tc, COL_TILE), lambda b, n: (row(b, n), CT_GQ)),
            pl.BlockSpec((tc, COL_TILE), lambda b, n: (row(b, n), CT_GK)),
            pl.BlockSpec((tc, 2 * COL_TILE), lambda b, n: (row(b, n), CT_GV // 2)),
            pl.BlockSpec((tc, 2 * COL_TILE), lambda b, n: (row(b, n), CT_GR // 2)),
            pl.BlockSpec((tc, LANES), lambda b, n: (row(b, n), 0)),
            pl.BlockSpec((LANES, GLA_HEADS * GLA_KEY_DIM), lambda b, n: (0, 0)),
            pl.BlockSpec((1, GLA_HEADS * GLA_KEY_DIM), lambda b, n: (0, 0)),
            pl.BlockSpec((1, GLA_VAL_DIM), lambda b, n: (0, 0)),
            pl.BlockSpec((tc, tc), lambda b, n: (0, 0)),
        ],
        out_specs=pl.BlockSpec((tc, GLA_HEADS * GLA_VAL_DIM), lambda b, n: (row(b, n), 0)),
        out_shape=jax.ShapeDtypeStruct((batch * seq, GLA_HEADS * GLA_VAL_DIM), BF16),
        scratch_shapes=[pltpu.VMEM((GLA_HEADS, GLA_VAL_DIM, GLA_KEY_DIM), F32)],
        compiler_params=_params(("parallel", "arbitrary")),
        name="gla",
    )(proj, proj, proj, proj, glr, wg, bg, gn, tri)


def _rms(x, g):
    ms = jnp.mean(x * x, axis=-1, keepdims=True)
    return x * lax.rsqrt(ms + NORM_EPS) * g


def _post_kernel(x_ref, o1_ref, o2_ref, o3_ref, s1_ref, s2_ref, s3_ref, gla_ref, ga_ref, gb_ref,
                 wa_ref, wb_ref, wo_ref, g2_ref, wi_ref, wd_ref, gf_ref, out_ref, *, sub_rows,
                 ffn_bounds):
    o_refs = (o1_ref, o2_ref, o3_ref)
    s_refs = (s1_ref, s2_ref, s3_ref)
    for r0 in range(0, x_ref.shape[0], sub_rows):
        rows = slice(r0, r0 + sub_rows)
        stats = [s[rows, :] for s in s_refs]
        heads = []
        for h in range(HEADS_PER_GROUP):
            hs = slice(h * HEAD_DIM, (h + 1) * HEAD_DIM)
            ms = [s[:, h:h + 1] for s in stats]
            ls = [s[:, HEADS_PER_GROUP + h:HEADS_PER_GROUP + h + 1] for s in stats]
            m_all = jnp.maximum(jnp.maximum(ms[0], ms[1]), ms[2])
            ws = [l * jnp.exp(m - m_all) for m, l in zip(ms, ls)]
            inv = 1.0 / (ws[0] + ws[1] + ws[2])
            acc = (ws[0] * inv) * o_refs[0][rows, hs].astype(F32)
            acc = acc + (ws[1] * inv) * o_refs[1][rows, hs].astype(F32)
            acc = acc + (ws[2] * inv) * o_refs[2][rows, hs].astype(F32)
            heads.append(acc.astype(BF16))
        attn = jnp.concatenate(heads, axis=1)

        ya = jnp.dot(attn, wa_ref[...], preferred_element_type=F32)
        yb = jnp.dot(gla_ref[rows, :], wb_ref[...], preferred_element_type=F32)
        mix = (jax.nn.sigmoid(ga_ref[rows, :].astype(F32)) * ya
               + jax.nn.sigmoid(gb_ref[rows, :].astype(F32)) * yb)
        x1 = x_ref[rows, :] + jnp.dot(mix.astype(BF16), wo_ref[...], preferred_element_type=F32)

        h2 = _rms(x1, g2_ref[...]).astype(BF16)
        acc = x1
        for lo, hi in zip(ffn_bounds[:-1], ffn_bounds[1:]):
            g = jnp.dot(h2, wi_ref[:, lo:hi], preferred_element_type=F32)
            u = jnp.dot(h2, wi_ref[:, FFN_HIDDEN + lo:FFN_HIDDEN + hi], preferred_element_type=F32)
            a = (g * jax.nn.sigmoid(g) * u).astype(BF16)
            acc = acc + jnp.dot(a, wd_ref[lo:hi, :], preferred_element_type=F32)
        out_ref[rows, :] = _rms(acc, gf_ref[...])


def _post(x2d, attn_o, attn_st, gla_out, proj, wa, wb, wo, g2, wi, wd, gf, tm, sub_rows, ffn_bounds):
    t = x2d.shape[0]
    resident = lambda shape: pl.BlockSpec(shape, lambda i: (0, 0), pipeline_mode=pl.Buffered(1))
    rows = lambda width, col=0: pl.BlockSpec((tm, width), lambda i: (i, col))
    return pl.pallas_call(
        functools.partial(_post_kernel, sub_rows=sub_rows, ffn_bounds=ffn_bounds),
        grid=(t // tm,),
        in_specs=[
            rows(D_MODEL),
            rows(GROUP_WIDTH), rows(GROUP_WIDTH), rows(GROUP_WIDTH),
            rows(LANES), rows(LANES), rows(LANES),
            rows(D_MODEL),
            rows(D_MODEL, CT_GA // 2), rows(D_MODEL, CT_GB // 2),
            resident(wa.shape), resident(wb.shape), resident(wo.shape), resident(g2.shape),
            resident(wi.shape), resident(wd.shape), resident(gf.shape),
        ],
        out_specs=rows(D_MODEL),
        out_shape=jax.ShapeDtypeStruct((t, D_MODEL), F32),
        compiler_params=_params(("parallel",)),
        name="post",
    )(x2d, *attn_o, *attn_st, gla_out, proj, proj, wa, wb, wo, g2, wi, wd, gf)


def _rope_tables(seq):
    half = ROPE_DIM // 2
    inv_freq = ROPE_THETA ** (-jnp.arange(0, ROPE_DIM, 2, dtype=F32) / ROPE_DIM)
    ang = jnp.arange(seq, dtype=F32)[:, None] * inv_freq[None, :]
    ang = jnp.concatenate([ang, ang], axis=-1)
    cos, sin = jnp.cos(ang), jnp.sin(ang)
    cos_t = jnp.concatenate([cos, jnp.ones((seq, HEAD_DIM - ROPE_DIM), F32)], axis=-1)
    sina_t = jnp.concatenate([-sin[:, :half], jnp.zeros((seq, HEAD_DIM - half), F32)], axis=-1)
    sinb_t = jnp.concatenate([jnp.zeros((seq, half), F32), sin[:, half:],
                              jnp.zeros((seq, HEAD_DIM - ROPE_DIM), F32)], axis=-1)
    return cos_t, sina_t, sinb_t


def _block_tril(n, blk):
    r = np.arange(n)
    return jnp.asarray((r[:, None] // blk == r[None, :] // blk) & (r[None, :] <= r[:, None]), BF16)


def _layer(x2d, batch, seq, norm1_g, w_in, w_gate_lr, b_gate, gla_norm_g, w_branch_a, w_branch_b,
           w_out, norm2_g, w_ffn_in, w_ffn_down, out_g, tables):
    a_w = 3 * GROUP_WIDTH
    gk_w = GLA_HEADS * GLA_KEY_DIM
    gv_w = GLA_HEADS * GLA_VAL_DIM
    o_aq, o_ak, o_av = 0, a_w, 2 * a_w
    o_gq = 3 * a_w
    o_gk = o_gq + gk_w
    o_gv = o_gk + gk_w
    o_gr = o_gv + gv_w
    o_glr = o_gr + gv_w
    o_ga = o_glr + GLA_GATE_RANK
    o_gb = o_ga + D_MODEL
    cols = lambda o, w: w_in[:, o:o + w]
    qkv = lambda g: [cols(o + g * GROUP_WIDTH, GROUP_WIDTH) for o in (o_aq, o_ak, o_av)]
    w_main = jnp.concatenate(
        [cols(o_gv, gv_w), cols(o_gr, gv_w), cols(o_ga, D_MODEL), cols(o_gb, D_MODEL)]
        + qkv(0) + [cols(o_gq, gk_w), cols(o_gk, gk_w)] + qkv(1) + qkv(2), axis=1).astype(BF16)
    w_glr = jnp.pad(cols(o_glr, GLA_GATE_RANK), ((0, 0), (0, LANES - GLA_GATE_RANK))).astype(BF16)
    wg = jnp.pad(w_gate_lr, ((0, LANES - GLA_GATE_RANK), (0, 0))).astype(BF16)

    proj, glr, qkv2, qkv3 = _inproj(x2d, norm1_g[None, :], w_main, w_glr, *tables,
                                    batch=batch, seq=seq, tm=256)

    qkv1 = proj.reshape(batch, 1, seq, MAIN_WIDTH)
    attn = [
        _attention_group(qkv1, (CT_Q1, CT_K1, CT_V1), 1, batch, seq, tile_positions=512),
        _attention_group(qkv2, (0, 1, 2), 4, batch, seq, tile_positions=2048),
        _attention_group(qkv3, (0, 1, 2), 16, batch, seq, tile_positions=2048),
    ]
    attn_o = [o for o, _ in attn]
    attn_st = [st for _, st in attn]

    tc = 256
    gla_out = _gla(proj, glr, wg, b_gate[None, :], gla_norm_g[None, :], _block_tril(tc, 128),
                   batch, seq, tc=tc, chunk=128)

    return _post(x2d, attn_o, attn_st, gla_out, proj,
                 w_branch_a.astype(BF16), w_branch_b.astype(BF16), w_out.astype(BF16),
                 norm2_g[None, :], w_ffn_in.astype(BF16), w_ffn_down.astype(BF16), out_g[None, :],
                 tm=512, sub_rows=256, ffn_bounds=(0, 1536, FFN_HIDDEN))


def kernel(x, norm1_g, w_in, w_gate_lr, b_gate, gla_norm_g, w_branch_a, w_branch_b, w_out, norm2_g,
           w_ffn_in, w_ffn_down, norm_f_g):
    batch, seq, d = x.shape
    depth = w_in.shape[0]
    assert depth == 1 and d == D_MODEL
    tables = _rope_tables(seq)
    x2d = x.reshape(batch * seq, d)
    out = _layer(x2d, batch, seq, norm1_g[0], w_in[0], w_gate_lr[0], b_gate[0], gla_norm_g[0],
                 w_branch_a[0], w_branch_b[0], w_out[0], norm2_g[0], w_ffn_in[0], w_ffn_down[0],
                 norm_f_g, tables)
    return out.reshape(batch, seq, d)
```

```python
import functools

import jax
import jax.numpy as jnp
import numpy as np
from jax import lax
from jax.experimental import pallas as pl
from jax.experimental.pallas import tpu as pltpu

F32 = jnp.float32
BF16 = jnp.bfloat16

D_MODEL = 1024
ATTN_GROUPS = ((128, 1), (512, 4), (2048, 16))
HEADS_PER_GROUP = 4
HEAD_DIM = 128
GROUP_WIDTH = HEADS_PER_GROUP * HEAD_DIM
KEYS_BACK = 128
BLOCK_HEADS_PER_TRIP = 16
ROPE_THETA = 500000.0
ROPE_DIM = HEAD_DIM // 4
GLA_HEADS = 4
GLA_KEY_DIM = 128
GLA_VAL_DIM = 256
GLA_GATE_RANK = 16
GLA_GATE_NORMALIZER = 16.0
GLA_SUB = 64
FFN_HIDDEN = 2816
NORM_EPS = 1e-6

LANES = 128
VMEM_LIMIT_BYTES = 56 * 1024 * 1024

COL_TILE = 512
CT_GV, CT_GR, CT_GA, CT_GB = 0, 2, 4, 6
CT_Q1, CT_K1, CT_V1 = 8, 9, 10
CT_GQ, CT_GK = 11, 12
N_MAIN_TILES = 13
MAIN_WIDTH = N_MAIN_TILES * COL_TILE
CT_G2, CT_G3 = 13, 16
N_COL_TILES = 19
QKV_WIDTH = 3 * COL_TILE


def _params(semantics):
    return pltpu.CompilerParams(dimension_semantics=semantics, vmem_limit_bytes=VMEM_LIMIT_BYTES)


def _inproj_kernel(x_ref, g_ref, w_ref, wglr_ref, p4_ref, p16_ref, t1_ref, t4_ref, t16_ref,
                   out_ref, glr_ref, d2_ref, d3_ref, h_sc, *, sub_rows):
    dot = functools.partial(jnp.dot, preferred_element_type=F32)
    tm = x_ref.shape[0]
    x = x_ref[...]
    ms = jnp.mean(x * x, axis=-1, keepdims=True)
    h = (x * lax.rsqrt(ms + NORM_EPS) * g_ref[...]).astype(BF16)
    h_sc[0] = h
    glr_ref[...] = dot(h, wglr_ref[...]).astype(BF16)
    for r0 in range(0, tm, sub_rows):
        rows = slice(r0, r0 + sub_rows)
        h_sc[1, rows, :] = dot(p4_ref[...], h[rows, :]).astype(BF16)
        h_sc[2, rows, :] = dot(p16_ref[...], h[rows, :]).astype(BF16)

    plans = ((CT_G3, 3, 16, d3_ref, 2, t16_ref), (CT_G2, 3, 4, d2_ref, 1, t4_ref),
             (CT_Q1, 3, 1, out_ref, 0, t1_ref), (CT_GQ, 2, 1, out_ref, 0, None),
             (0, CT_Q1, 1, out_ref, 0, None))
    for first, count, dilation, dst_ref, order, tab_ref in plans:
        per = sub_rows // dilation
        for jt in range(count):
            j = first + jt
            acc = dot(h_sc[order], w_ref[:, j * COL_TILE:(j + 1) * COL_TILE])
            for hh in range(COL_TILE // HEAD_DIM):
                a = acc[:, hh * HEAD_DIM:(hh + 1) * HEAD_DIM]
                if tab_ref is not None and jt < 2:
                    up = pltpu.roll(a, HEAD_DIM - ROPE_DIM // 2, 1)
                    dn = pltpu.roll(a, ROPE_DIM // 2, 1)
                    a = a * tab_ref[0] + up * tab_ref[1] + dn * tab_ref[2]
                a = a.astype(BF16)
                if dilation == 1:
                    lo = j * COL_TILE + hh * HEAD_DIM
                    dst_ref[:, lo:lo + HEAD_DIM] = a
                else:
                    lo = jt * COL_TILE + hh * HEAD_DIM
                    for s in range(tm // sub_rows):
                        for c in range(dilation):
                            src = s * sub_rows + c * per
                            dst_ref[0, c, s * per:(s + 1) * per, lo:lo + HEAD_DIM] = a[src:src + per, :]


def _phase_major_perm(n, dilation):
    dst = np.arange(n)
    c, i = dst // (n // dilation), dst % (n // dilation)
    p = np.zeros((n, n), np.float32)
    p[dst, dilation * i + c] = 1.0
    return jnp.asarray(p, BF16)


def _phase_major_rows(table, sub_rows, dilation):
    seq, lanes = table.shape
    t = table.reshape(seq // sub_rows, sub_rows // dilation, dilation, lanes)
    return t.transpose(0, 2, 1, 3).reshape(seq, lanes)


def _inproj(x2d, g1, w_main, w_glr, tables, batch, seq, tm, sub_rows):
    t = x2d.shape[0]
    pos_blocks = seq // tm
    resident = lambda shape: pl.BlockSpec(shape, lambda i: (0, 0), pipeline_mode=pl.Buffered(1))
    table_spec = pl.BlockSpec((3, tm, LANES), lambda i: (0, i % pos_blocks, 0))
    tab1 = jnp.stack(tables)
    tab4 = jnp.stack([_phase_major_rows(tb, sub_rows, 4) for tb in tables])
    tab16 = jnp.stack([_phase_major_rows(tb, sub_rows, 16) for tb in tables])
    p4, p16 = _phase_major_perm(sub_rows, 4), _phase_major_perm(sub_rows, 16)

    def dilated_spec(dilation):
        return pl.BlockSpec((1, dilation, tm // dilation, QKV_WIDTH),
                            lambda i: (i // pos_blocks, 0, i % pos_blocks, 0))

    return pl.pallas_call(
        functools.partial(_inproj_kernel, sub_rows=sub_rows),
        grid=(t // tm,),
        in_specs=[
            pl.BlockSpec((tm, D_MODEL), lambda i: (i, 0)),
            resident((1, D_MODEL)),
            resident(w_main.shape),
            resident(w_glr.shape),
            resident(p4.shape), resident(p16.shape),
            table_spec, table_spec, table_spec,
        ],
        out_specs=[
            pl.BlockSpec((tm, MAIN_WIDTH), lambda i: (i, 0)),
            pl.BlockSpec((tm, LANES), lambda i: (i, 0)),
            dilated_spec(4),
            dilated_spec(16),
        ],
        out_shape=[
            jax.ShapeDtypeStruct((t, MAIN_WIDTH), BF16),
            jax.ShapeDtypeStruct((t, LANES), BF16),
            jax.ShapeDtypeStruct((batch, 4, seq // 4, QKV_WIDTH), BF16),
            jax.ShapeDtypeStruct((batch, 16, seq // 16, QKV_WIDTH), BF16),
        ],
        scratch_shapes=[pltpu.VMEM((3, tm, D_MODEL), BF16)],
        compiler_params=_params(("parallel",)),
        name="inproj",
    )(x2d, g1, w_main, w_glr, p4, p16, tab1, tab4, tab16)


def _attn_kernel(q_ref, kp_ref, kc_ref, vp_ref, vc_ref, o_ref, st_ref, *scratch, tq, dilation):
    n = pl.program_id(1)
    blk = KEYS_BACK
    qi = lax.broadcasted_iota(jnp.int32, (blk, 2 * blk), 0)
    jj = lax.broadcasted_iota(jnp.int32, (blk, 2 * blk), 1)
    band = jnp.logical_and(jj >= qi, jj <= qi + blk)
    first_band = jnp.logical_and(band, jj + n * tq >= blk)
    lane = lax.broadcasted_iota(jnp.int32, (blk, LANES), 1)
    scale = HEAD_DIM ** -0.5
    neg_inf = jnp.float32(-jnp.inf)

    def phase(c):
        for qb in range(tq // blk):
            rows = slice(qb * blk, (qb + 1) * blk)
            valid = first_band if qb == 0 else band
            stats = jnp.zeros((blk, LANES), F32)
            nat = pl.ds(c + qb * blk * dilation, blk, stride=dilation)
            for h in range(HEADS_PER_GROUP):
                hs = slice(h * HEAD_DIM, (h + 1) * HEAD_DIM)
                q = q_ref[0, c, rows, hs]
                if qb == 0:
                    kk = jnp.concatenate([kp_ref[0, c, :, hs], kc_ref[0, c, 0:blk, hs]], axis=0)
                    vv = jnp.concatenate([vp_ref[0, c, :, hs], vc_ref[0, c, 0:blk, hs]], axis=0)
                else:
                    kk = kc_ref[0, c, (qb - 1) * blk:(qb + 1) * blk, hs]
                    vv = vc_ref[0, c, (qb - 1) * blk:(qb + 1) * blk, hs]
                s = lax.dot_general(q, kk, (((1,), (1,)), ((), ())), preferred_element_type=F32)
                s = jnp.where(valid, s * scale, neg_inf)
                m = jnp.max(s, axis=-1, keepdims=True)
                p = jnp.exp(s - m)
                l = jnp.sum(p, axis=-1, keepdims=True)
                o = jnp.dot(p.astype(BF16), vv, preferred_element_type=F32) / l
                if dilation == 1:
                    o_ref[0, rows, hs] = o.astype(BF16)
                else:
                    scratch[0][h, nat, :] = o
                stats = jnp.where(lane == h, m, stats)
                stats = jnp.where(lane == HEADS_PER_GROUP + h, l, stats)
            if dilation == 1:
                st_ref[0, rows, :] = stats
            else:
                st_ref[0, nat, :] = stats

    if dilation == 1:
        phase(0)
    else:
        per_trip = max(1, BLOCK_HEADS_PER_TRIP // (HEADS_PER_GROUP * (tq // blk)))

        def body(t, carry):
            for u in range(per_trip):
                phase(t * per_trip + u)
            return carry
        lax.fori_loop(0, dilation // per_trip, body, 0)
        for h in range(HEADS_PER_GROUP):
            o_ref[0, :, h * HEAD_DIM:(h + 1) * HEAD_DIM] = scratch[0][h].astype(BF16)


def _attention_group(qkv, col_tiles, dilation, batch, seq, tile_positions):
    r = dilation
    tq = tile_positions // r
    nblk = seq // tile_positions
    prev_per_blk = tq // KEYS_BACK
    cq, ck, cv = col_tiles

    def cur(ct):
        return pl.BlockSpec((1, r, tq, GROUP_WIDTH), lambda b, n: (b, 0, n, ct))

    def prev(ct):
        return pl.BlockSpec((1, r, KEYS_BACK, GROUP_WIDTH),
                            lambda b, n: (b, 0, jnp.maximum(n * prev_per_blk - 1, 0), ct))

    scratch = []
    if r > 1:
        scratch.append(pltpu.VMEM((HEADS_PER_GROUP, tile_positions, HEAD_DIM), F32))
    o, st = pl.pallas_call(
        functools.partial(_attn_kernel, tq=tq, dilation=r),
        grid=(batch, nblk),
        in_specs=[cur(cq), prev(ck), cur(ck), prev(cv), cur(cv)],
        out_specs=[
            pl.BlockSpec((1, tile_positions, GROUP_WIDTH), lambda b, n: (b, n, 0)),
            pl.BlockSpec((1, tile_positions, LANES), lambda b, n: (b, n, 0)),
        ],
        out_shape=[
            jax.ShapeDtypeStruct((batch, seq, GROUP_WIDTH), BF16),
            jax.ShapeDtypeStruct((batch, seq, LANES), F32),
        ],
        scratch_shapes=scratch,
        compiler_params=_params(("parallel", "arbitrary")),
        name=f"attn_r{r}",
    )(qkv, qkv, qkv, qkv, qkv)
    return o.reshape(batch * seq, GROUP_WIDTH), st.reshape(batch * seq, LANES)


def _gla_kernel(q_ref, k_ref, v_ref, gr_ref, glr_ref, wg_ref, bg_ref, gn_ref, tri_ref,
                o_ref, state_sc, *, tc, chunk):
    n = pl.program_id(1)

    @pl.when(n == 0)
    def _():
        state_sc[...] = jnp.zeros_like(state_sc)

    z = jnp.dot(glr_ref[...], wg_ref[...], preferred_element_type=F32) + bg_ref[...]
    log_a = (jnp.minimum(z, 0.0) - jnp.log(1.0 + jnp.exp(-jnp.abs(z)))) / GLA_GATE_NORMALIZER
    hi = log_a.astype(BF16)
    lo = (log_a - hi.astype(F32)).astype(BF16)
    tri = tri_ref[...]
    bcum = (jnp.dot(tri, hi, preferred_element_type=F32)
            + jnp.dot(tri, lo, preferred_element_type=F32))

    sub = GLA_SUB
    nsub = chunk // sub
    sub_shift = sub.bit_length() - 1
    row = lax.broadcasted_iota(jnp.int32, (chunk, chunk), 0)
    colm = lax.broadcasted_iota(jnp.int32, (chunk, chunk), 1)
    diag_mask = jnp.logical_and((row >> sub_shift) == (colm >> sub_shift), colm <= row)
    rsub = lax.broadcasted_iota(jnp.int32, (chunk, GLA_KEY_DIM), 0) >> sub_shift
    q_scale = GLA_KEY_DIM ** -0.5
    dn_t = (((1,), (1,)), ((), ()))
    dn_l = (((0,), (0,)), ((), ()))

    for h in range(GLA_HEADS):
        ks = slice(h * GLA_KEY_DIM, (h + 1) * GLA_KEY_DIM)
        vs = slice(h * GLA_VAL_DIM, (h + 1) * GLA_VAL_DIM)
        for c in range(tc // chunk):
            rows = slice(c * chunk, (c + 1) * chunk)
            b = bcum[rows, ks]
            q = q_ref[rows, ks].astype(F32) * q_scale
            k = k_ref[rows, ks].astype(F32)
            v = v_ref[rows, vs]
            centre = b[sub // 2:sub // 2 + 1, :]
            for sb in range(1, nsub):
                centre = jnp.where(rsub == sb, b[sb * sub + sub // 2:sb * sub + sub // 2 + 1, :], centre)
            qc = (q * jnp.exp(b - centre)).astype(BF16)
            kc = (k * jnp.exp(centre - b)).astype(BF16)
            a = jnp.where(diag_mask, lax.dot_general(qc, kc, dn_t, preferred_element_type=F32), 0.0)
            for sb in range(1, nsub):
                lo_r, hi_r = sb * sub, (sb + 1) * sub
                bound = b[lo_r - 1:lo_r, :]
                qb = (q[lo_r:hi_r] * jnp.exp(b[lo_r:hi_r] - bound)).astype(BF16)
                kb = (k[0:lo_r] * jnp.exp(bound - b[0:lo_r])).astype(BF16)
                kb = jnp.concatenate([kb, jnp.zeros((chunk - lo_r, GLA_KEY_DIM), BF16)], axis=0)
                off = lax.dot_general(qb, kb, dn_t, preferred_element_type=F32)
                pieces = []
                if lo_r:
                    pieces.append(jnp.zeros((lo_r, chunk), F32))
                pieces.append(off)
                if chunk - hi_r:
                    pieces.append(jnp.zeros((chunk - hi_r, chunk), F32))
                a = a + jnp.concatenate(pieces, axis=0)
            b_last = b[chunk - 1:chunk, :]
            q_in = (q * jnp.exp(b)).astype(BF16)
            k_st = (k * jnp.exp(b_last - b)).astype(BF16)
            st = state_sc[h]
            o = (jnp.dot(a.astype(BF16), v, preferred_element_type=F32)
                 + lax.dot_general(q_in, st.astype(BF16), dn_t, preferred_element_type=F32))
            state_sc[h] = jnp.exp(b_last) * st + lax.dot_general(v, k_st, dn_l,
                                                                   preferred_element_type=F32)
            ms = jnp.mean(o * o, axis=-1, keepdims=True)
            y = o * lax.rsqrt(ms + NORM_EPS) * gn_ref[...]
            g = gr_ref[rows, vs].astype(F32)
            o_ref[rows, vs] = (y * (g * jax.nn.sigmoid(g))).astype(BF16)


def _gla(proj, glr, wg, bg, gn, tri, batch, seq, tc, chunk):
    nblk = seq // tc
    row = lambda b, n: b * nblk + n
    return pl.pallas_call(
        functools.partial(_gla_kernel, tc=tc, chunk=chunk),
        grid=(batch, nblk),
        in_specs=[
            pl.BlockSpec((tc, COL_TILE), lambda b, n: (row(b, n), CT_GQ)),
            pl.BlockSpec((tc, COL_TILE), lambda b, n: (row(b, n), CT_GK)),
            pl.BlockSpec((tc, 2 * COL_TILE), lambda b, n: (row(b, n), CT_GV // 2)),
            pl.BlockSpec((tc, 2 * COL_TILE), lambda b, n: (row(b, n), CT_GR // 2)),
            pl.BlockSpec((tc, LANES), lambda b, n: (row(b, n), 0)),
            pl.BlockSpec((LANES, GLA_HEADS * GLA_KEY_DIM), lambda b, n: (0, 0)),
            pl.BlockSpec((1, GLA_HEADS * GLA_KEY_DIM), lambda b, n: (0, 0)),
            pl.BlockSpec((1, GLA_VAL_DIM), lambda b, n: (0, 0)),
            pl.BlockSpec((tc, tc), lambda b, n: (0, 0)),
        ],
        out_specs=pl.BlockSpec((tc, GLA_HEADS * GLA_VAL_DIM), lambda b, n: (row(b, n), 0)),
        out_shape=jax.ShapeDtypeStruct((batch * seq, GLA_HEADS * GLA_VAL_DIM), BF16),
        scratch_shapes=[pltpu.VMEM((GLA_HEADS, GLA_VAL_DIM, GLA_KEY_DIM), F32)],
        compiler_params=_params(("parallel", "arbitrary")),
        name="gla",
    )(proj, proj, proj, proj, glr, wg, bg, gn, tri)


def _rms(x, g):
    ms = jnp.mean(x * x, axis=-1, keepdims=True)
    return x * lax.rsqrt(ms + NORM_EPS) * g


def _post_kernel(x_ref, o1_ref, o2_ref, o3_ref, s1_ref, s2_ref, s3_ref, gla_ref, ga_ref, gb_ref,
                 wa_ref, wb_ref, wo_ref, g2_ref, wi_ref, wd_ref, gf_ref, out_ref, *, sub_rows,
                 ffn_bounds):
    o_refs = (o1_ref, o2_ref, o3_ref)
    s_refs = (s1_ref, s2_ref, s3_ref)
    subs = [slice(r0, r0 + sub_rows) for r0 in range(0, x_ref.shape[0], sub_rows)]
    chunks = list(zip(ffn_bounds[:-1], ffn_bounds[1:]))
    dot = functools.partial(jnp.dot, preferred_element_type=F32)

    def gla_branch(rows):
        return dot(gla_ref[rows, :], wb_ref[...])

    def attn_branch(rows):
        stats = [s[rows, :] for s in s_refs]
        heads = []
        for h in range(HEADS_PER_GROUP):
            hs = slice(h * HEAD_DIM, (h + 1) * HEAD_DIM)
            ms = [s[:, h:h + 1] for s in stats]
            ls = [s[:, HEADS_PER_GROUP + h:HEADS_PER_GROUP + h + 1] for s in stats]
            m_all = jnp.maximum(jnp.maximum(ms[0], ms[1]), ms[2])
            ws = [l * jnp.exp(m - m_all) for m, l in zip(ms, ls)]
            inv = 1.0 / (ws[0] + ws[1] + ws[2])
            acc = (ws[0] * inv) * o_refs[0][rows, hs].astype(F32)
            acc = acc + (ws[1] * inv) * o_refs[1][rows, hs].astype(F32)
            acc = acc + (ws[2] * inv) * o_refs[2][rows, hs].astype(F32)
            heads.append(acc.astype(BF16))
        return dot(jnp.concatenate(heads, axis=1), wa_ref[...])

    def mixer(rows, ya, yb):
        mix = (jax.nn.sigmoid(ga_ref[rows, :].astype(F32)) * ya
               + jax.nn.sigmoid(gb_ref[rows, :].astype(F32)) * yb)
        return x_ref[rows, :] + dot(mix.astype(BF16), wo_ref[...])

    def ffn_up(h2, lo, hi):
        g = dot(h2, wi_ref[:, lo:hi])
        u = dot(h2, wi_ref[:, FFN_HIDDEN + lo:FFN_HIDDEN + hi])
        return (g * jax.nn.sigmoid(g) * u).astype(BF16)

    yb = [gla_branch(rows) for rows in subs]
    ya = [attn_branch(rows) for rows in subs]
    x1 = [mixer(rows, a, b) for rows, a, b in zip(subs, ya, yb)]
    h2 = [_rms(v, g2_ref[...]).astype(BF16) for v in x1]
    acc = x1
    for lo, hi in chunks:
        act = [ffn_up(h, lo, hi) for h in h2]
        acc = [v + dot(a, wd_ref[lo:hi, :]) for v, a in zip(acc, act)]
    for rows, v in zip(subs, acc):
        out_ref[rows, :] = _rms(v, gf_ref[...])


def _post(x2d, attn_o, attn_st, gla_out, proj, wa, wb, wo, g2, wi, wd, gf, tm, sub_rows, ffn_bounds):
    t = x2d.shape[0]
    resident = lambda shape: pl.BlockSpec(shape, lambda i: (0, 0), pipeline_mode=pl.Buffered(1))
    rows = lambda width, col=0: pl.BlockSpec((tm, width), lambda i: (i, col))
    return pl.pallas_call(
        functools.partial(_post_kernel, sub_rows=sub_rows, ffn_bounds=ffn_bounds),
        grid=(t // tm,),
        in_specs=[
            rows(D_MODEL),
            rows(GROUP_WIDTH), rows(GROUP_WIDTH), rows(GROUP_WIDTH),
            rows(LANES), rows(LANES), rows(LANES),
            rows(D_MODEL),
            rows(D_MODEL, CT_GA // 2), rows(D_MODEL, CT_GB // 2),
            resident(wa.shape), resident(wb.shape), resident(wo.shape), resident(g2.shape),
            resident(wi.shape), resident(wd.shape), resident(gf.shape),
        ],
        out_specs=rows(D_MODEL),
        out_shape=jax.ShapeDtypeStruct((t, D_MODEL), F32),
        compiler_params=_params(("parallel",)),
        name="post",
    )(x2d, *attn_o, *attn_st, gla_out, proj, proj, wa, wb, wo, g2, wi, wd, gf)


def _rope_tables(seq):
    half = ROPE_DIM // 2
    inv_freq = ROPE_THETA ** (-jnp.arange(0, ROPE_DIM, 2, dtype=F32) / ROPE_DIM)
    ang = jnp.arange(seq, dtype=F32)[:, None] * inv_freq[None, :]
    ang = jnp.concatenate([ang, ang], axis=-1)
    cos, sin = jnp.cos(ang), jnp.sin(ang)
    cos_t = jnp.concatenate([cos, jnp.ones((seq, HEAD_DIM - ROPE_DIM), F32)], axis=-1)
    sina_t = jnp.concatenate([-sin[:, :half], jnp.zeros((seq, HEAD_DIM - half), F32)], axis=-1)
    sinb_t = jnp.concatenate([jnp.zeros((seq, half), F32), sin[:, half:],
                              jnp.zeros((seq, HEAD_DIM - ROPE_DIM), F32)], axis=-1)
    return cos_t, sina_t, sinb_t


def _block_tril(n, blk):
    r = np.arange(n)
    return jnp.asarray((r[:, None] // blk == r[None, :] // blk) & (r[None, :] <= r[:, None]), BF16)


def _layer(x2d, batch, seq, norm1_g, w_in, w_gate_lr, b_gate, gla_norm_g, w_branch_a, w_branch_b,
           w_out, norm2_g, w_ffn_in, w_ffn_down, out_g, tables):
    a_w = 3 * GROUP_WIDTH
    gk_w = GLA_HEADS * GLA_KEY_DIM
    gv_w = GLA_HEADS * GLA_VAL_DIM
    o_aq, o_ak, o_av = 0, a_w, 2 * a_w
    o_gq = 3 * a_w
    o_gk = o_gq + gk_w
    o_gv = o_gk + gk_w
    o_gr = o_gv + gv_w
    o_glr = o_gr + gv_w
    o_ga = o_glr + GLA_GATE_RANK
    o_gb = o_ga + D_MODEL
    cols = lambda o, w: w_in[:, o:o + w]
    qkv = lambda g: [cols(o + g * GROUP_WIDTH, GROUP_WIDTH) for o in (o_aq, o_ak, o_av)]
    w_main = jnp.concatenate(
        [cols(o_gv, gv_w), cols(o_gr, gv_w), cols(o_ga, D_MODEL), cols(o_gb, D_MODEL)]
        + qkv(0) + [cols(o_gq, gk_w), cols(o_gk, gk_w)] + qkv(1) + qkv(2), axis=1).astype(BF16)
    w_glr = jnp.pad(cols(o_glr, GLA_GATE_RANK), ((0, 0), (0, LANES - GLA_GATE_RANK))).astype(BF16)
    wg = jnp.pad(w_gate_lr, ((0, LANES - GLA_GATE_RANK), (0, 0))).astype(BF16)

    proj, glr, qkv2, qkv3 = _inproj(x2d, norm1_g[None, :], w_main, w_glr, tables,
                                    batch=batch, seq=seq, tm=512, sub_rows=256)

    qkv1 = proj.reshape(batch, 1, seq, MAIN_WIDTH)
    attn = [
        _attention_group(qkv1, (CT_Q1, CT_K1, CT_V1), 1, batch, seq, tile_positions=512),
        _attention_group(qkv2, (0, 1, 2), 4, batch, seq, tile_positions=2048),
        _attention_group(qkv3, (0, 1, 2), 16, batch, seq, tile_positions=2048),
    ]
    attn_o = [o for o, _ in attn]
    attn_st = [st for _, st in attn]

    tc = 256
    gla_out = _gla(proj, glr, wg, b_gate[None, :], gla_norm_g[None, :], _block_tril(tc, 128),
                   batch, seq, tc=tc, chunk=128)

    return _post(x2d, attn_o, attn_st, gla_out, proj,
                 w_branch_a.astype(BF16), w_branch_b.astype(BF16), w_out.astype(BF16),
                 norm2_g[None, :], w_ffn_in.astype(BF16), w_ffn_down.astype(BF16), out_g[None, :],
                 tm=512, sub_rows=256, ffn_bounds=(0, 1536, FFN_HIDDEN))


def kernel(x, norm1_g, w_in, w_gate_lr, b_gate, gla_norm_g, w_branch_a, w_branch_b, w_out, norm2_g,
           w_ffn_in, w_ffn_down, norm_f_g):
    batch, seq, d = x.shape
    depth = w_in.shape[0]
    assert depth == 1 and d == D_MODEL
    tables = _rope_tables(seq)
    x2d = x.reshape(batch * seq, d)
    out = _layer(x2d, batch, seq, norm1_g[0], w_in[0], w_gate_lr[0], b_gate[0], gla_norm_g[0],
                 w_branch_a[0], w_branch_b[0], w_out[0], norm2_g[0], w_ffn_in[0], w_ffn_down[0],
                 norm_f_g, tables)
    return out.reshape(batch, seq, d)
```

```python
import functools

import jax
import jax.numpy as jnp
import numpy as np
from jax import lax
from jax.experimental import pallas as pl
from jax.experimental.pallas import tpu as pltpu

F32 = jnp.float32
BF16 = jnp.bfloat16

D_MODEL = 1024
ATTN_GROUPS = ((128, 1), (512, 4), (2048, 16))
HEADS_PER_GROUP = 4
HEAD_DIM = 128
GROUP_WIDTH = HEADS_PER_GROUP * HEAD_DIM
KEYS_BACK = 128
BLOCK_HEADS_PER_TRIP = 16
ROPE_THETA = 500000.0
ROPE_DIM = HEAD_DIM // 4
GLA_HEADS = 4
GLA_KEY_DIM = 128
GLA_VAL_DIM = 256
GLA_GATE_RANK = 16
GLA_GATE_NORMALIZER = 16.0
GLA_SUB = 64
FFN_HIDDEN = 2816
NORM_EPS = 1e-6

LANES = 128
VMEM_LIMIT_BYTES = 56 * 1024 * 1024

COL_TILE = 512
CT_GV, CT_GR, CT_GA, CT_GB = 0, 2, 4, 6
CT_Q1, CT_K1, CT_V1 = 8, 9, 10
CT_GQ, CT_GK = 11, 12
N_MAIN_TILES = 13
MAIN_WIDTH = N_MAIN_TILES * COL_TILE
CT_G2, CT_G3 = 13, 16
N_COL_TILES = 19
QKV_WIDTH = 3 * COL_TILE


def _params(semantics):
    return pltpu.CompilerParams(dimension_semantics=semantics, vmem_limit_bytes=VMEM_LIMIT_BYTES)


def _inproj_kernel(x_ref, g_ref, w_ref, wglr_ref, p4_ref, p16_ref, t1_ref, t4_ref, t16_ref,
                   out_ref, glr_ref, d2_ref, d3_ref, h_sc, *, sub_rows):
    dot = functools.partial(jnp.dot, preferred_element_type=F32)
    tm = x_ref.shape[0]
    x = x_ref[...]
    ms = jnp.mean(x * x, axis=-1, keepdims=True)
    h = (x * lax.rsqrt(ms + NORM_EPS) * g_ref[...]).astype(BF16)
    h_sc[0] = h
    glr_ref[...] = dot(h, wglr_ref[...]).astype(BF16)
    for r0 in range(0, tm, sub_rows):
        rows = slice(r0, r0 + sub_rows)
        h_sc[1, rows, :] = dot(p4_ref[...], h[rows, :]).astype(BF16)
        h_sc[2, rows, :] = dot(p16_ref[...], h[rows, :]).astype(BF16)

    plans = ((CT_G3, 3, 16, d3_ref, 2, t16_ref), (CT_G2, 3, 4, d2_ref, 1, t4_ref),
             (CT_Q1, 3, 1, out_ref, 0, t1_ref), (CT_GQ, 2, 1, out_ref, 0, None),
             (0, CT_Q1, 1, out_ref, 0, None))
    for first, count, dilation, dst_ref, order, tab_ref in plans:
        per = sub_rows // dilation
        for jt in range(count):
            j = first + jt
            acc = dot(h_sc[order], w_ref[:, j * COL_TILE:(j + 1) * COL_TILE])
            for hh in range(COL_TILE // HEAD_DIM):
                a = acc[:, hh * HEAD_DIM:(hh + 1) * HEAD_DIM]
                if tab_ref is not None and jt < 2:
                    up = pltpu.roll(a, HEAD_DIM - ROPE_DIM // 2, 1)
                    dn = pltpu.roll(a, ROPE_DIM // 2, 1)
                    a = a * tab_ref[0] + up * tab_ref[1] + dn * tab_ref[2]
                a = a.astype(BF16)
                if dilation == 1:
                    lo = j * COL_TILE + hh * HEAD_DIM
                    dst_ref[:, lo:lo + HEAD_DIM] = a
                else:
                    lo = jt * COL_TILE + hh * HEAD_DIM
                    for s in range(tm // sub_rows):
                        for c in range(dilation):
                            src = s * sub_rows + c * per
                            dst_ref[0, c, s * per:(s + 1) * per, lo:lo + HEAD_DIM] = a[src:src + per, :]


def _phase_major_perm(n, dilation):
    dst = np.arange(n)
    c, i = dst // (n // dilation), dst % (n // dilation)
    p = np.zeros((n, n), np.float32)
    p[dst, dilation * i + c] = 1.0
    return jnp.asarray(p, BF16)


def _phase_major_rows(table, sub_rows, dilation):
    seq, lanes = table.shape
    t = table.reshape(seq // sub_rows, sub_rows // dilation, dilation, lanes)
    return t.transpose(0, 2, 1, 3).reshape(seq, lanes)


def _inproj(x2d, g1, w_main, w_glr, tables, batch, seq, tm, sub_rows):
    t = x2d.shape[0]
    pos_blocks = seq // tm
    resident = lambda shape: pl.BlockSpec(shape, lambda i: (0, 0), pipeline_mode=pl.Buffered(1))
    table_spec = pl.BlockSpec((3, tm, LANES), lambda i: (0, i % pos_blocks, 0))
    tab1 = jnp.stack(tables)
    tab4 = jnp.stack([_phase_major_rows(tb, sub_rows, 4) for tb in tables])
    tab16 = jnp.stack([_phase_major_rows(tb, sub_rows, 16) for tb in tables])
    p4, p16 = _phase_major_perm(sub_rows, 4), _phase_major_perm(sub_rows, 16)

    def dilated_spec(dilation):
        return pl.BlockSpec((1, dilation, tm // dilation, QKV_WIDTH),
                            lambda i: (i // pos_blocks, 0, i % pos_blocks, 0))

    return pl.pallas_call(
        functools.partial(_inproj_kernel, sub_rows=sub_rows),
        grid=(t // tm,),
        in_specs=[
            pl.BlockSpec((tm, D_MODEL), lambda i: (i, 0)),
            resident((1, D_MODEL)),
            resident(w_main.shape),
            resident(w_glr.shape),
            resident(p4.shape), resident(p16.shape),
            table_spec, table_spec, table_spec,
        ],
        out_specs=[
            pl.BlockSpec((tm, MAIN_WIDTH), lambda i: (i, 0)),
            pl.BlockSpec((tm, LANES), lambda i: (i, 0)),
            dilated_spec(4),
            dilated_spec(16),
        ],
        out_shape=[
            jax.ShapeDtypeStruct((t, MAIN_WIDTH), BF16),
            jax.ShapeDtypeStruct((t, LANES), BF16),
            jax.ShapeDtypeStruct((batch, 4, seq // 4, QKV_WIDTH), BF16),
            jax.ShapeDtypeStruct((batch, 16, seq // 16, QKV_WIDTH), BF16),
        ],
        scratch_shapes=[pltpu.VMEM((3, tm, D_MODEL), BF16)],
        compiler_params=_params(("parallel",)),
        name="inproj",
    )(x2d, g1, w_main, w_glr, p4, p16, tab1, tab4, tab16)


def _attn_kernel(q_ref, kp_ref, kc_ref, vp_ref, vc_ref, o_ref, st_ref, *scratch, tq, dilation):
    n = pl.program_id(1)
    blk = KEYS_BACK
    qi = lax.broadcasted_iota(jnp.int32, (blk, 2 * blk), 0)
    jj = lax.broadcasted_iota(jnp.int32, (blk, 2 * blk), 1)
    band = jnp.logical_and(jj >= qi, jj <= qi + blk)
    first_band = jnp.logical_and(band, jj + n * tq >= blk)
    lane = lax.broadcasted_iota(jnp.int32, (blk, LANES), 1)
    scale = HEAD_DIM ** -0.5
    neg_inf = jnp.float32(-jnp.inf)

    def windows(c, qb, cur_ref, prev_ref, hs):
        if qb == 0:
            return jnp.concatenate([prev_ref[0, c, :, hs], cur_ref[0, c, 0:blk, hs]], axis=0)
        return cur_ref[0, c, (qb - 1) * blk:(qb + 1) * blk, hs]

    def scores(c, qb):
        valid = first_band if qb == 0 else band
        out = []
        for h in range(HEADS_PER_GROUP):
            hs = slice(h * HEAD_DIM, (h + 1) * HEAD_DIM)
            q = q_ref[0, c, qb * blk:(qb + 1) * blk, hs]
            kk = windows(c, qb, kc_ref, kp_ref, hs)
            s = lax.dot_general(q, kk, (((1,), (1,)), ((), ())), preferred_element_type=F32)
            out.append(jnp.where(valid, s * scale, neg_inf))
        return out

    def finish(c, qb, s_heads):
        rows = slice(qb * blk, (qb + 1) * blk)
        nat = pl.ds(c + qb * blk * dilation, blk, stride=dilation)
        stats = jnp.zeros((blk, LANES), F32)
        for h, s in enumerate(s_heads):
            hs = slice(h * HEAD_DIM, (h + 1) * HEAD_DIM)
            vv = windows(c, qb, vc_ref, vp_ref, hs)
            m = jnp.max(s, axis=-1, keepdims=True)
            p = jnp.exp(s - m)
            l = jnp.sum(p, axis=-1, keepdims=True)
            o = jnp.dot(p.astype(BF16), vv, preferred_element_type=F32) / l
            if dilation == 1:
                o_ref[0, rows, hs] = o.astype(BF16)
            else:
                scratch[0][h, nat, :] = o
            stats = jnp.where(lane == h, m, stats)
            stats = jnp.where(lane == HEADS_PER_GROUP + h, l, stats)
        if dilation == 1:
            st_ref[0, rows, :] = stats
        else:
            st_ref[0, nat, :] = stats

    def run(blocks):
        pending = None
        for c, qb in blocks:
            s_heads = scores(c, qb)
            if pending is not None:
                finish(*pending)
            pending = (c, qb, s_heads)
        finish(*pending)

    if dilation == 1:
        run([(0, qb) for qb in range(tq // blk)])
    else:
        per_trip = max(1, BLOCK_HEADS_PER_TRIP // (HEADS_PER_GROUP * (tq // blk)))

        def body(t, carry):
            run([(t * per_trip + u, qb) for u in range(per_trip) for qb in range(tq // blk)])
            return carry
        lax.fori_loop(0, dilation // per_trip, body, 0)
        for h in range(HEADS_PER_GROUP):
            o_ref[0, :, h * HEAD_DIM:(h + 1) * HEAD_DIM] = scratch[0][h].astype(BF16)


def _attention_group(qkv, col_tiles, dilation, batch, seq, tile_positions):
    r = dilation
    tq = tile_positions // r
    nblk = seq // tile_positions
    prev_per_blk = tq // KEYS_BACK
    cq, ck, cv = col_tiles

    def cur(ct):
        return pl.BlockSpec((1, r, tq, GROUP_WIDTH), lambda b, n: (b, 0, n, ct))

    def prev(ct):
        return pl.BlockSpec((1, r, KEYS_BACK, GROUP_WIDTH),
                            lambda b, n: (b, 0, jnp.maximum(n * prev_per_blk - 1, 0), ct))

    scratch = []
    if r > 1:
        scratch.append(pltpu.VMEM((HEADS_PER_GROUP, tile_positions, HEAD_DIM), F32))
    o, st = pl.pallas_call(
        functools.partial(_attn_kernel, tq=tq, dilation=r),
        grid=(batch, nblk),
        in_specs=[cur(cq), prev(ck), cur(ck), prev(cv), cur(cv)],
        out_specs=[
            pl.BlockSpec((1, tile_positions, GROUP_WIDTH), lambda b, n: (b, n, 0)),
            pl.BlockSpec((1, tile_positions, LANES), lambda b, n: (b, n, 0)),
        ],
        out_shape=[
            jax.ShapeDtypeStruct((batch, seq, GROUP_WIDTH), BF16),
            jax.ShapeDtypeStruct((batch, seq, LANES), F32),
        ],
        scratch_shapes=scratch,
        compiler_params=_params(("parallel", "arbitrary")),
        name=f"attn_r{r}",
    )(qkv, qkv, qkv, qkv, qkv)
    return o.reshape(batch * seq, GROUP_WIDTH), st.reshape(batch * seq, LANES)


def _gla_kernel(q_ref, k_ref, v_ref, gr_ref, glr_ref, wg_ref, bg_ref, gn_ref, tri_ref,
                o_ref, state_sc, *, tc, chunk):
    n = pl.program_id(1)

    @pl.when(n == 0)
    def _():
        state_sc[...] = jnp.zeros_like(state_sc)

    z = jnp.dot(glr_ref[...], wg_ref[...], preferred_element_type=F32) + bg_ref[...]
    log_a = (jnp.minimum(z, 0.0) - jnp.log(1.0 + jnp.exp(-jnp.abs(z)))) / GLA_GATE_NORMALIZER
    hi = log_a.astype(BF16)
    lo = (log_a - hi.astype(F32)).astype(BF16)
    tri = tri_ref[...]
    bcum = (jnp.dot(tri, hi, preferred_element_type=F32)
            + jnp.dot(tri, lo, preferred_element_type=F32))

    sub = GLA_SUB
    nsub = chunk // sub
    sub_shift = sub.bit_length() - 1
    row = lax.broadcasted_iota(jnp.int32, (chunk, chunk), 0)
    colm = lax.broadcasted_iota(jnp.int32, (chunk, chunk), 1)
    diag_mask = jnp.logical_and((row >> sub_shift) == (colm >> sub_shift), colm <= row)
    rsub = lax.broadcasted_iota(jnp.int32, (chunk, GLA_KEY_DIM), 0) >> sub_shift
    q_scale = GLA_KEY_DIM ** -0.5
    dn_t = (((1,), (1,)), ((), ()))
    dn_l = (((0,), (0,)), ((), ()))

    for c in range(tc // chunk):
        rows = slice(c * chunk, (c + 1) * chunk)
        for h in range(GLA_HEADS):
            ks = slice(h * GLA_KEY_DIM, (h + 1) * GLA_KEY_DIM)
            vs = slice(h * GLA_VAL_DIM, (h + 1) * GLA_VAL_DIM)
            b = bcum[rows, ks]
            q = q_ref[rows, ks].astype(F32) * q_scale
            k = k_ref[rows, ks].astype(F32)
            v = v_ref[rows, vs]
            centre = b[sub // 2:sub // 2 + 1, :]
            for sb in range(1, nsub):
                centre = jnp.where(rsub == sb, b[sb * sub + sub // 2:sb * sub + sub // 2 + 1, :], centre)
            qc = (q * jnp.exp(b - centre)).astype(BF16)
            kc = (k * jnp.exp(centre - b)).astype(BF16)
            a = jnp.where(diag_mask, lax.dot_general(qc, kc, dn_t, preferred_element_type=F32), 0.0)
            for sb in range(1, nsub):
                lo_r, hi_r = sb * sub, (sb + 1) * sub
                bound = b[lo_r - 1:lo_r, :]
                qb = (q[lo_r:hi_r] * jnp.exp(b[lo_r:hi_r] - bound)).astype(BF16)
                kb = (k[0:lo_r] * jnp.exp(bound - b[0:lo_r])).astype(BF16)
                kb = jnp.concatenate([kb, jnp.zeros((chunk - lo_r, GLA_KEY_DIM), BF16)], axis=0)
                off = lax.dot_general(qb, kb, dn_t, preferred_element_type=F32)
                pieces = []
                if lo_r:
                    pieces.append(jnp.zeros((lo_r, chunk), F32))
                pieces.append(off)
                if chunk - hi_r:
                    pieces.append(jnp.zeros((chunk - hi_r, chunk), F32))
                a = a + jnp.concatenate(pieces, axis=0)
            b_last = b[chunk - 1:chunk, :]
            q_in = (q * jnp.exp(b)).astype(BF16)
            k_st = (k * jnp.exp(b_last - b)).astype(BF16)
            st = state_sc[h]
            o_inter = lax.dot_general(q_in, st.astype(BF16), dn_t, preferred_element_type=F32)
            state_sc[h] = jnp.exp(b_last) * st + lax.dot_general(v, k_st, dn_l,
                                                                   preferred_element_type=F32)
            o = o_inter + jnp.dot(a.astype(BF16), v, preferred_element_type=F32)
            ms = jnp.mean(o * o, axis=-1, keepdims=True)
            y = o * lax.rsqrt(ms + NORM_EPS) * gn_ref[...]
            g = gr_ref[rows, vs].astype(F32)
            o_ref[rows, vs] = (y * (g * jax.nn.sigmoid(g))).astype(BF16)


def _gla(proj, glr, wg, bg, gn, tri, batch, seq, tc, chunk):
    nblk = seq // tc
    row = lambda b, n: b * nblk + n
    return pl.pallas_call(
        functools.partial(_gla_kernel, tc=tc, chunk=chunk),
        grid=(batch, nblk),
        in_specs=[
            pl.BlockSpec((tc, COL_TILE), lambda b, n: (row(b, n), CT_GQ)),
            pl.BlockSpec((tc, COL_TILE), lambda b, n: (row(b, n), CT_GK)),
            pl.BlockSpec((tc, 2 * COL_TILE), lambda b, n: (row(b, n), CT_GV // 2)),
            pl.BlockSpec((tc, 2 * COL_TILE), lambda b, n: (row(b, n), CT_GR // 2)),
            pl.BlockSpec((tc, LANES), lambda b, n: (row(b, n), 0)),
            pl.BlockSpec((LANES, GLA_HEADS * GLA_KEY_DIM), lambda b, n: (0, 0)),
            pl.BlockSpec((1, GLA_HEADS * GLA_KEY_DIM), lambda b, n: (0, 0)),
            pl.BlockSpec((1, GLA_VAL_DIM), lambda b, n: (0, 0)),
            pl.BlockSpec((tc, tc), lambda b, n: (0, 0)),
        ],
        out_specs=pl.BlockSpec((tc, GLA_HEADS * GLA_VAL_DIM), lambda b, n: (row(b, n), 0)),
        out_shape=jax.ShapeDtypeStruct((batch * seq, GLA_HEADS * GLA_VAL_DIM), BF16),
        scratch_shapes=[pltpu.VMEM((GLA_HEADS, GLA_VAL_DIM, GLA_KEY_DIM), F32)],
        compiler_params=_params(("parallel", "arbitrary")),
        name="gla",
    )(proj, proj, proj, proj, glr, wg, bg, gn, tri)


def _rms(x, g):
    ms = jnp.mean(x * x, axis=-1, keepdims=True)
    return x * lax.rsqrt(ms + NORM_EPS) * g


def _post_kernel(x_ref, o1_ref, o2_ref, o3_ref, s1_ref, s2_ref, s3_ref, gla_ref, ga_ref, gb_ref,
                 wa_ref, wb_ref, wo_ref, g2_ref, wi_ref, wd_ref, gf_ref, out_ref, *, sub_rows,
                 ffn_bounds):
    o_refs = (o1_ref, o2_ref, o3_ref)
    s_refs = (s1_ref, s2_ref, s3_ref)
    subs = [slice(r0, r0 + sub_rows) for r0 in range(0, x_ref.shape[0], sub_rows)]
    chunks = list(zip(ffn_bounds[:-1], ffn_bounds[1:]))
    dot = functools.partial(jnp.dot, preferred_element_type=F32)

    def gla_branch(rows):
        return dot(gla_ref[rows, :], wb_ref[...])

    def attn_branch(rows):
        stats = [s[rows, :] for s in s_refs]
        heads = []
        for h in range(HEADS_PER_GROUP):
            hs = slice(h * HEAD_DIM, (h + 1) * HEAD_DIM)
            ms = [s[:, h:h + 1] for s in stats]
            ls = [s[:, HEADS_PER_GROUP + h:HEADS_PER_GROUP + h + 1] for s in stats]
            m_all = jnp.maximum(jnp.maximum(ms[0], ms[1]), ms[2])
            ws = [l * jnp.exp(m - m_all) for m, l in zip(ms, ls)]
            inv = 1.0 / (ws[0] + ws[1] + ws[2])
            acc = (ws[0] * inv) * o_refs[0][rows, hs].astype(F32)
            acc = acc + (ws[1] * inv) * o_refs[1][rows, hs].astype(F32)
            acc = acc + (ws[2] * inv) * o_refs[2][rows, hs].astype(F32)
            heads.append(acc.astype(BF16))
        return dot(jnp.concatenate(heads, axis=1), wa_ref[...])

    def mixer(rows, ya, yb):
        mix = (jax.nn.sigmoid(ga_ref[rows, :].astype(F32)) * ya
               + jax.nn.sigmoid(gb_ref[rows, :].astype(F32)) * yb)
        return x_ref[rows, :] + dot(mix.astype(BF16), wo_ref[...])

    def ffn_up(h2, lo, hi):
        g = dot(h2, wi_ref[:, lo:hi])
        u = dot(h2, wi_ref[:, FFN_HIDDEN + lo:FFN_HIDDEN + hi])
        return (g * jax.nn.sigmoid(g) * u).astype(BF16)

    yb = [gla_branch(rows) for rows in subs]
    ya = [attn_branch(rows) for rows in subs]
    x1 = [mixer(rows, a, b) for rows, a, b in zip(subs, ya, yb)]
    h2 = [_rms(v, g2_ref[...]).astype(BF16) for v in x1]
    acc = x1
    for lo, hi in chunks:
        act = [ffn_up(h, lo, hi) for h in h2]
        acc = [v + dot(a, wd_ref[lo:hi, :]) for v, a in zip(acc, act)]
    for rows, v in zip(subs, acc):
        out_ref[rows, :] = _rms(v, gf_ref[...])


def _post(x2d, attn_o, attn_st, gla_out, proj, wa, wb, wo, g2, wi, wd, gf, tm, sub_rows, ffn_bounds):
    t = x2d.shape[0]
    resident = lambda shape: pl.BlockSpec(shape, lambda i: (0, 0), pipeline_mode=pl.Buffered(1))
    rows = lambda width, col=0: pl.BlockSpec((tm, width), lambda i: (i, col))
    return pl.pallas_call(
        functools.partial(_post_kernel, sub_rows=sub_rows, ffn_bounds=ffn_bounds),
        grid=(t // tm,),
        in_specs=[
            rows(D_MODEL),
            rows(GROUP_WIDTH), rows(GROUP_WIDTH), rows(GROUP_WIDTH),
            rows(LANES), rows(LANES), rows(LANES),
            rows(D_MODEL),
            rows(D_MODEL, CT_GA // 2), rows(D_MODEL, CT_GB // 2),
            resident(wa.shape), resident(wb.shape), resident(wo.shape), resident(g2.shape),
            resident(wi.shape), resident(wd.shape), resident(gf.shape),
        ],
        out_specs=rows(D_MODEL),
        out_shape=jax.ShapeDtypeStruct((t, D_MODEL), F32),
        compiler_params=_params(("parallel",)),
        name="post",
    )(x2d, *attn_o, *attn_st, gla_out, proj, proj, wa, wb, wo, g2, wi, wd, gf)


def _rope_tables(seq):
    half = ROPE_DIM // 2
    inv_freq = ROPE_THETA ** (-jnp.arange(0, ROPE_DIM, 2, dtype=F32) / ROPE_DIM)
    ang = jnp.arange(seq, dtype=F32)[:, None] * inv_freq[None, :]
    ang = jnp.concatenate([ang, ang], axis=-1)
    cos, sin = jnp.cos(ang), jnp.sin(ang)
    cos_t = jnp.concatenate([cos, jnp.ones((seq, HEAD_DIM - ROPE_DIM), F32)], axis=-1)
    sina_t = jnp.concatenate([-sin[:, :half], jnp.zeros((seq, HEAD_DIM - half), F32)], axis=-1)
    sinb_t = jnp.concatenate([jnp.zeros((seq, half), F32), sin[:, half:],
                              jnp.zeros((seq, HEAD_DIM - ROPE_DIM), F32)], axis=-1)
    return cos_t, sina_t, sinb_t


def _block_tril(n, blk):
    r = np.arange(n)
    return jnp.asarray((r[:, None] // blk == r[None, :] // blk) & (r[None, :] <= r[:, None]), BF16)


def _layer(x2d, batch, seq, norm1_g, w_in, w_gate_lr, b_gate, gla_norm_g, w_branch_a, w_branch_b,
           w_out, norm2_g, w_ffn_in, w_ffn_down, out_g, tables):
    a_w = 3 * GROUP_WIDTH
    gk_w = GLA_HEADS * GLA_KEY_DIM
    gv_w = GLA_HEADS * GLA_VAL_DIM
    o_aq, o_ak, o_av = 0, a_w, 2 * a_w
    o_gq = 3 * a_w
    o_gk = o_gq + gk_w
    o_gv = o_gk + gk_w
    o_gr = o_gv + gv_w
    o_glr = o_gr + gv_w
    o_ga = o_glr + GLA_GATE_RANK
    o_gb = o_ga + D_MODEL
    cols = lambda o, w: w_in[:, o:o + w]
    qkv = lambda g: [cols(o + g * GROUP_WIDTH, GROUP_WIDTH) for o in (o_aq, o_ak, o_av)]
    w_main = jnp.concatenate(
        [cols(o_gv, gv_w), cols(o_gr, gv_w), cols(o_ga, D_MODEL), cols(o_gb, D_MODEL)]
        + qkv(0) + [cols(o_gq, gk_w), cols(o_gk, gk_w)] + qkv(1) + qkv(2), axis=1).astype(BF16)
    w_glr = jnp.pad(cols(o_glr, GLA_GATE_RANK), ((0, 0), (0, LANES - GLA_GATE_RANK))).astype(BF16)
    wg = jnp.pad(w_gate_lr, ((0, LANES - GLA_GATE_RANK), (0, 0))).astype(BF16)

    proj, glr, qkv2, qkv3 = _inproj(x2d, norm1_g[None, :], w_main, w_glr, tables,
                                    batch=batch, seq=seq, tm=512, sub_rows=256)

    qkv1 = proj.reshape(batch, 1, seq, MAIN_WIDTH)
    attn = [
        _attention_group(qkv1, (CT_Q1, CT_K1, CT_V1), 1, batch, seq, tile_positions=512),
        _attention_group(qkv2, (0, 1, 2), 4, batch, seq, tile_positions=2048),
        _attention_group(qkv3, (0, 1, 2), 16, batch, seq, tile_positions=2048),
    ]
    attn_o = [o for o, _ in attn]
    attn_st = [st for _, st in attn]

    tc = 256
    gla_out = _gla(proj, glr, wg, b_gate[None, :], gla_norm_g[None, :], _block_tril(tc, 128),
                   batch, seq, tc=tc, chunk=128)

    return _post(x2d, attn_o, attn_st, gla_out, proj,
                 w_branch_a.astype(BF16), w_branch_b.astype(BF16), w_out.astype(BF16),
                 norm2_g[None, :], w_ffn_in.astype(BF16), w_ffn_down.astype(BF16), out_g[None, :],
                 tm=512, sub_rows=256, ffn_bounds=(0, 1536, FFN_HIDDEN))


def kernel(x, norm1_g, w_in, w_gate_lr, b_gate, gla_norm_g, w_branch_a, w_branch_b, w_out, norm2_g,
           w_ffn_in, w_ffn_down, norm_f_g):
    batch, seq, d = x.shape
    depth = w_in.shape[0]
    assert depth == 1 and d == D_MODEL
    tables = _rope_tables(seq)
    x2d = x.reshape(batch * seq, d)
    out = _layer(x2d, batch, seq, norm1_g[0], w_in[0], w_gate_lr[0], b_gate[0], gla_norm_g[0],
                 w_branch_a[0], w_branch_b[0], w_out[0], norm2_g[0], w_ffn_in[0], w_ffn_down[0],
                 norm_f_g, tables)
    return out.reshape(batch, seq, d)
```

```python
import functools

import jax
import jax.numpy as jnp
import numpy as np
from jax import lax
from jax.experimental import pallas as pl
from jax.experimental.pallas import tpu as pltpu

F32 = jnp.float32
BF16 = jnp.bfloat16

D_MODEL = 1024
ATTN_GROUPS = ((128, 1), (512, 4), (2048, 16))
HEADS_PER_GROUP = 4
HEAD_DIM = 128
GROUP_WIDTH = HEADS_PER_GROUP * HEAD_DIM
KEYS_BACK = 128
BLOCK_HEADS_PER_TRIP = 16
ROPE_THETA = 500000.0
ROPE_DIM = HEAD_DIM // 4
GLA_HEADS = 4
GLA_KEY_DIM = 128
assert GLA_KEY_DIM == HEAD_DIM
QUERY_SCALE = HEAD_DIM ** -0.5
GLA_VAL_DIM = 256
GLA_GATE_RANK = 16
GLA_GATE_NORMALIZER = 16.0
GLA_SUB = 64
FFN_HIDDEN = 2816
NORM_EPS = 1e-6

LANES = 128
VMEM_LIMIT_BYTES = 56 * 1024 * 1024

COL_TILE = 512
CT_GV, CT_GR, CT_GA, CT_GB = 0, 2, 4, 6
CT_Q1, CT_K1, CT_V1 = 8, 9, 10
CT_GQ, CT_GK = 11, 12
N_MAIN_TILES = 13
MAIN_WIDTH = N_MAIN_TILES * COL_TILE
CT_G2, CT_G3 = 13, 16
N_COL_TILES = 19
QKV_WIDTH = 3 * COL_TILE


def _params(semantics):
    return pltpu.CompilerParams(dimension_semantics=semantics, vmem_limit_bytes=VMEM_LIMIT_BYTES)


def _inproj_kernel(x_ref, g_ref, w_ref, wglr_ref, p4_ref, p16_ref, t1_ref, t4_ref, t16_ref,
                   out_ref, glr_ref, d2_ref, d3_ref, h_sc, *, sub_rows):
    dot = functools.partial(jnp.dot, preferred_element_type=F32)
    tm = x_ref.shape[0]
    x = x_ref[...]
    ms = jnp.mean(x * x, axis=-1, keepdims=True)
    h = (x * lax.rsqrt(ms + NORM_EPS) * g_ref[...]).astype(BF16)
    h_sc[0] = h
    glr_ref[...] = dot(h, wglr_ref[...]).astype(BF16)
    for r0 in range(0, tm, sub_rows):
        rows = slice(r0, r0 + sub_rows)
        h_sc[1, rows, :] = dot(p4_ref[...], h[rows, :]).astype(BF16)
        h_sc[2, rows, :] = dot(p16_ref[...], h[rows, :]).astype(BF16)

    plans = ((CT_G3, 3, 16, d3_ref, 2, t16_ref, 0), (CT_G2, 3, 4, d2_ref, 1, t4_ref, 0),
             (CT_Q1, 3, 1, out_ref, 0, t1_ref, 0), (CT_GQ, 2, 1, out_ref, 0, None, 0),
             (0, CT_Q1, 1, out_ref, 0, None, None))
    for first, count, dilation, dst_ref, order, tab_ref, q_tile in plans:
        per = sub_rows // dilation
        for jt in range(count):
            j = first + jt
            acc = dot(h_sc[order], w_ref[:, j * COL_TILE:(j + 1) * COL_TILE])
            for hh in range(COL_TILE // HEAD_DIM):
                a = acc[:, hh * HEAD_DIM:(hh + 1) * HEAD_DIM]
                if tab_ref is not None and jt < 2:
                    up = pltpu.roll(a, HEAD_DIM - ROPE_DIM // 2, 1)
                    dn = pltpu.roll(a, ROPE_DIM // 2, 1)
                    a = a * tab_ref[0] + up * tab_ref[1] + dn * tab_ref[2]
                if jt == q_tile:
                    a = a * QUERY_SCALE
                a = a.astype(BF16)
                if dilation == 1:
                    lo = j * COL_TILE + hh * HEAD_DIM
                    dst_ref[:, lo:lo + HEAD_DIM] = a
                else:
                    lo = jt * COL_TILE + hh * HEAD_DIM
                    for s in range(tm // sub_rows):
                        for c in range(dilation):
                            src = s * sub_rows + c * per
                            dst_ref[0, c, s * per:(s + 1) * per, lo:lo + HEAD_DIM] = a[src:src + per, :]


def _phase_major_perm(n, dilation):
    dst = np.arange(n)
    c, i = dst // (n // dilation), dst % (n // dilation)
    p = np.zeros((n, n), np.float32)
    p[dst, dilation * i + c] = 1.0
    return jnp.asarray(p, BF16)


def _phase_major_rows(table, sub_rows, dilation):
    seq, lanes = table.shape
    t = table.reshape(seq // sub_rows, sub_rows // dilation, dilation, lanes)
    return t.transpose(0, 2, 1, 3).reshape(seq, lanes)


def _inproj(x2d, g1, w_main, w_glr, tables, batch, seq, tm, sub_rows):
    t = x2d.shape[0]
    pos_blocks = seq // tm
    resident = lambda shape: pl.BlockSpec(shape, lambda i: (0, 0), pipeline_mode=pl.Buffered(1))
    table_spec = pl.BlockSpec((3, tm, LANES), lambda i: (0, i % pos_blocks, 0))
    tab1 = jnp.stack(tables)
    tab4 = jnp.stack([_phase_major_rows(tb, sub_rows, 4) for tb in tables])
    tab16 = jnp.stack([_phase_major_rows(tb, sub_rows, 16) for tb in tables])
    p4, p16 = _phase_major_perm(sub_rows, 4), _phase_major_perm(sub_rows, 16)

    def dilated_spec(dilation):
        return pl.BlockSpec((1, dilation, tm // dilation, QKV_WIDTH),
                            lambda i: (i // pos_blocks, 0, i % pos_blocks, 0))

    return pl.pallas_call(
        functools.partial(_inproj_kernel, sub_rows=sub_rows),
        grid=(t // tm,),
        in_specs=[
            pl.BlockSpec((tm, D_MODEL), lambda i: (i, 0)),
            resident((1, D_MODEL)),
            resident(w_main.shape),
            resident(w_glr.shape),
            resident(p4.shape), resident(p16.shape),
            table_spec, table_spec, table_spec,
        ],
        out_specs=[
            pl.BlockSpec((tm, MAIN_WIDTH), lambda i: (i, 0)),
            pl.BlockSpec((tm, LANES), lambda i: (i, 0)),
            dilated_spec(4),
            dilated_spec(16),
        ],
        out_shape=[
            jax.ShapeDtypeStruct((t, MAIN_WIDTH), BF16),
            jax.ShapeDtypeStruct((t, LANES), BF16),
            jax.ShapeDtypeStruct((batch, 4, seq // 4, QKV_WIDTH), BF16),
            jax.ShapeDtypeStruct((batch, 16, seq // 16, QKV_WIDTH), BF16),
        ],
        scratch_shapes=[pltpu.VMEM((3, tm, D_MODEL), BF16)],
        compiler_params=_params(("parallel",)),
        name="inproj",
    )(x2d, g1, w_main, w_glr, p4, p16, tab1, tab4, tab16)


def _attn_kernel(q_ref, kp_ref, kc_ref, vp_ref, vc_ref, o_ref, st_ref, *scratch, tq, dilation):
    n = pl.program_id(1)
    blk = KEYS_BACK
    qi = lax.broadcasted_iota(jnp.int32, (blk, 2 * blk), 0)
    jj = lax.broadcasted_iota(jnp.int32, (blk, 2 * blk), 1)
    band = jnp.logical_and(jj >= qi, jj <= qi + blk)
    first_band = jnp.logical_and(band, jj + n * tq >= blk)
    lane = lax.broadcasted_iota(jnp.int32, (blk, LANES), 1)
    neg_inf = jnp.float32(-jnp.inf)

    def windows(c, qb, cur_ref, prev_ref, hs):
        if qb == 0:
            return jnp.concatenate([prev_ref[0, c, :, hs], cur_ref[0, c, 0:blk, hs]], axis=0)
        return cur_ref[0, c, (qb - 1) * blk:(qb + 1) * blk, hs]

    def scores(c, qb):
        valid = first_band if qb == 0 else band
        out = []
        for h in range(HEADS_PER_GROUP):
            hs = slice(h * HEAD_DIM, (h + 1) * HEAD_DIM)
            q = q_ref[0, c, qb * blk:(qb + 1) * blk, hs]
            kk = windows(c, qb, kc_ref, kp_ref, hs)
            s = lax.dot_general(q, kk, (((1,), (1,)), ((), ())), preferred_element_type=F32)
            out.append(jnp.where(valid, s, neg_inf))
        return out

    def finish(c, qb, s_heads):
        rows = slice(qb * blk, (qb + 1) * blk)
        nat = pl.ds(c + qb * blk * dilation, blk, stride=dilation)
        stats = jnp.zeros((blk, LANES), F32)
        for h, s in enumerate(s_heads):
            hs = slice(h * HEAD_DIM, (h + 1) * HEAD_DIM)
            vv = windows(c, qb, vc_ref, vp_ref, hs)
            m = jnp.max(s, axis=-1, keepdims=True)
            p = jnp.exp(s - m)
            l = jnp.sum(p, axis=-1, keepdims=True)
            o = jnp.dot(p.astype(BF16), vv, preferred_element_type=F32) / l
            if dilation == 1:
                o_ref[0, rows, hs] = o.astype(BF16)
            else:
                scratch[0][h, nat, :] = o
            stats = jnp.where(lane == h, m, stats)
            stats = jnp.where(lane == HEADS_PER_GROUP + h, l, stats)
        if dilation == 1:
            st_ref[0, rows, :] = stats
        else:
            st_ref[0, nat, :] = stats

    def run(blocks):
        pending = None
        for c, qb in blocks:
            s_heads = scores(c, qb)
            if pending is not None:
                finish(*pending)
            pending = (c, qb, s_heads)
        finish(*pending)

    if dilation == 1:
        run([(0, qb) for qb in range(tq // blk)])
    else:
        per_trip = max(1, BLOCK_HEADS_PER_TRIP // (HEADS_PER_GROUP * (tq // blk)))

        def body(t, carry):
            run([(t * per_trip + u, qb) for u in range(per_trip) for qb in range(tq // blk)])
            return carry
        lax.fori_loop(0, dilation // per_trip, body, 0)
        for h in range(HEADS_PER_GROUP):
            o_ref[0, :, h * HEAD_DIM:(h + 1) * HEAD_DIM] = scratch[0][h].astype(BF16)


def _attention_group(qkv, col_tiles, dilation, batch, seq, tile_positions):
    r = dilation
    tq = tile_positions // r
    nblk = seq // tile_positions
    prev_per_blk = tq // KEYS_BACK
    cq, ck, cv = col_tiles

    def cur(ct):
        return pl.BlockSpec((1, r, tq, GROUP_WIDTH), lambda b, n: (b, 0, n, ct))

    def prev(ct):
        return pl.BlockSpec((1, r, KEYS_BACK, GROUP_WIDTH),
                            lambda b, n: (b, 0, jnp.maximum(n * prev_per_blk - 1, 0), ct))

    scratch = []
    if r > 1:
        scratch.append(pltpu.VMEM((HEADS_PER_GROUP, tile_positions, HEAD_DIM), F32))
    o, st = pl.pallas_call(
        functools.partial(_attn_kernel, tq=tq, dilation=r),
        grid=(batch, nblk),
        in_specs=[cur(cq), prev(ck), cur(ck), prev(cv), cur(cv)],
        out_specs=[
            pl.BlockSpec((1, tile_positions, GROUP_WIDTH), lambda b, n: (b, n, 0)),
            pl.BlockSpec((1, tile_positions, LANES), lambda b, n: (b, n, 0)),
        ],
        out_shape=[
            jax.ShapeDtypeStruct((batch, seq, GROUP_WIDTH), BF16),
            jax.ShapeDtypeStruct((batch, seq, LANES), F32),
        ],
        scratch_shapes=scratch,
        compiler_params=_params(("parallel", "arbitrary")),
        name=f"attn_r{r}",
    )(qkv, qkv, qkv, qkv, qkv)
    return o.reshape(batch * seq, GROUP_WIDTH), st.reshape(batch * seq, LANES)


def _gla_kernel(q_ref, k_ref, v_ref, gr_ref, glr_ref, wg_ref, bg_ref, gn_ref, tri_ref,
                o_ref, state_sc, *, blk_rows, chunk):
    n = pl.program_id(1)

    @pl.when(n == 0)
    def _():
        state_sc[...] = jnp.zeros_like(state_sc)

    sub = GLA_SUB
    nsub = chunk // sub
    sub_shift = sub.bit_length() - 1
    row = lax.broadcasted_iota(jnp.int32, (chunk, chunk), 0)
    colm = lax.broadcasted_iota(jnp.int32, (chunk, chunk), 1)
    diag_mask = jnp.logical_and((row >> sub_shift) == (colm >> sub_shift), colm <= row)
    dn_t = (((1,), (1,)), ((), ()))
    dn_l = (((0,), (0,)), ((), ()))
    subs = [slice(sb * sub, (sb + 1) * sub) for sb in range(nsub)]
    rows_cat = functools.partial(jnp.concatenate, axis=0)

    def chunk_head(r0, bcum, c, h):
        rows = pl.ds(pl.multiple_of(r0 + c * chunk, chunk), chunk)
        ks = slice(h * GLA_KEY_DIM, (h + 1) * GLA_KEY_DIM)
        vs = slice(h * GLA_VAL_DIM, (h + 1) * GLA_VAL_DIM)
        b = bcum[c * chunk:(c + 1) * chunk, ks]
        q = q_ref[rows, ks].astype(F32)
        k = k_ref[rows, ks].astype(F32)
        v = v_ref[rows, vs]
        cen = [b[sb * sub + sub // 2:sb * sub + sub // 2 + 1, :] for sb in range(nsub)]
        bnd = [None] + [b[sb * sub - 1:sb * sub, :] for sb in range(1, nsub)]
        b_last = b[chunk - 1:chunk, :]
        dev = [b[s, :] - cen[sb] for sb, s in enumerate(subs)]
        q_c = [q[s, :] * jnp.exp(dev[sb]) for sb, s in enumerate(subs)]
        k_c = [k[s, :] * jnp.exp(-dev[sb]) for sb, s in enumerate(subs)]
        a = jnp.where(diag_mask,
                      lax.dot_general(rows_cat(q_c).astype(BF16), rows_cat(k_c).astype(BF16), dn_t,
                                      preferred_element_type=F32), 0.0)
        for sb in range(1, nsub):
            qb = (q_c[sb] * jnp.exp(cen[sb] - bnd[sb])).astype(BF16)
            kb = [k_c[t] * jnp.exp(bnd[sb] - cen[t]) for t in range(sb)]
            kb.append(jnp.zeros((chunk - sb * sub, GLA_KEY_DIM), F32))
            off = lax.dot_general(qb, rows_cat(kb).astype(BF16), dn_t, preferred_element_type=F32)
            pieces = [jnp.zeros((sb * sub, chunk), F32), off]
            if chunk - (sb + 1) * sub:
                pieces.append(jnp.zeros((chunk - (sb + 1) * sub, chunk), F32))
            a = a + rows_cat(pieces)
        q_in = rows_cat([q_c[sb] * jnp.exp(cen[sb]) for sb in range(nsub)]).astype(BF16)
        k_st = rows_cat([k_c[sb] * jnp.exp(b_last - cen[sb]) for sb in range(nsub)]).astype(BF16)
        st = state_sc[h]
        o_inter = lax.dot_general(q_in, st.astype(BF16), dn_t, preferred_element_type=F32)
        state_sc[h] = jnp.exp(b_last) * st + lax.dot_general(v, k_st, dn_l,
                                                               preferred_element_type=F32)
        o = o_inter + jnp.dot(a.astype(BF16), v, preferred_element_type=F32)
        ms = jnp.mean(o * o, axis=-1, keepdims=True)
        y = o * lax.rsqrt(ms + NORM_EPS) * gn_ref[...]
        g = gr_ref[rows, vs].astype(F32)
        o_ref[rows, vs] = (y * (g * jax.nn.sigmoid(g))).astype(BF16)

    def block(i, carry):
        r0 = pl.multiple_of(i * blk_rows, blk_rows)
        z = jnp.dot(glr_ref[pl.ds(r0, blk_rows), :], wg_ref[...],
                    preferred_element_type=F32) + bg_ref[...]
        log_a = (jnp.minimum(z, 0.0) - jnp.log(1.0 + jnp.exp(-jnp.abs(z)))) / GLA_GATE_NORMALIZER
        hi = log_a.astype(BF16)
        lo = (log_a - hi.astype(F32)).astype(BF16)
        bcum = (jnp.dot(tri_ref[...], hi, preferred_element_type=F32)
                + jnp.dot(tri_ref[...], lo, preferred_element_type=F32))
        for c in range(blk_rows // chunk):
            for h in range(GLA_HEADS):
                chunk_head(r0, bcum, c, h)
        return carry

    lax.fori_loop(0, q_ref.shape[0] // blk_rows, block, 0)


def _gla(proj, glr, wg, bg, gn, batch, seq, tc, blk_rows, chunk):
    tri = _block_tril(blk_rows, chunk)
    nblk = seq // tc
    row = lambda b, n: b * nblk + n
    return pl.pallas_call(
        functools.partial(_gla_kernel, blk_rows=blk_rows, chunk=chunk),
        grid=(batch, nblk),
        in_specs=[
            pl.BlockSpec((tc, COL_TILE), lambda b, n: (row(b, n), CT_GQ)),
            pl.BlockSpec((tc, COL_TILE), lambda b, n: (row(b, n), CT_GK)),
            pl.BlockSpec((tc, 2 * COL_TILE), lambda b, n: (row(b, n), CT_GV // 2)),
            pl.BlockSpec((tc, 2 * COL_TILE), lambda b, n: (row(b, n), CT_GR // 2)),
            pl.BlockSpec((tc, LANES), lambda b, n: (row(b, n), 0)),
            pl.BlockSpec((LANES, GLA_HEADS * GLA_KEY_DIM), lambda b, n: (0, 0)),
            pl.BlockSpec((1, GLA_HEADS * GLA_KEY_DIM), lambda b, n: (0, 0)),
            pl.BlockSpec((1, GLA_VAL_DIM), lambda b, n: (0, 0)),
            pl.BlockSpec((blk_rows, blk_rows), lambda b, n: (0, 0)),
        ],
        out_specs=pl.BlockSpec((tc, GLA_HEADS * GLA_VAL_DIM), lambda b, n: (row(b, n), 0)),
        out_shape=jax.ShapeDtypeStruct((batch * seq, GLA_HEADS * GLA_VAL_DIM), BF16),
        scratch_shapes=[pltpu.VMEM((GLA_HEADS, GLA_VAL_DIM, GLA_KEY_DIM), F32)],
        compiler_params=_params(("parallel", "arbitrary")),
        name="gla",
    )(proj, proj, proj, proj, glr, wg, bg, gn, tri)


def _rms(x, g):
    ms = jnp.mean(x * x, axis=-1, keepdims=True)
    return x * lax.rsqrt(ms + NORM_EPS) * g


def _post_kernel(x_ref, o1_ref, o2_ref, o3_ref, s1_ref, s2_ref, s3_ref, gla_ref, ga_ref, gb_ref,
                 wa_ref, wb_ref, wo_ref, g2_ref, wi_ref, wd_ref, gf_ref, out_ref, *, sub_rows,
                 ffn_bounds):
    o_refs = (o1_ref, o2_ref, o3_ref)
    s_refs = (s1_ref, s2_ref, s3_ref)
    subs = [slice(r0, r0 + sub_rows) for r0 in range(0, x_ref.shape[0], sub_rows)]
    chunks = list(zip(ffn_bounds[:-1], ffn_bounds[1:]))
    dot = functools.partial(jnp.dot, preferred_element_type=F32)

    def gla_branch(rows):
        return dot(gla_ref[rows, :], wb_ref[...])

    def attn_branch(rows):
        stats = [s[rows, :] for s in s_refs]
        heads = []
        for h in range(HEADS_PER_GROUP):
            hs = slice(h * HEAD_DIM, (h + 1) * HEAD_DIM)
            ms = [s[:, h:h + 1] for s in stats]
            ls = [s[:, HEADS_PER_GROUP + h:HEADS_PER_GROUP + h + 1] for s in stats]
            m_all = jnp.maximum(jnp.maximum(ms[0], ms[1]), ms[2])
            ws = [l * jnp.exp(m - m_all) for m, l in zip(ms, ls)]
            inv = 1.0 / (ws[0] + ws[1] + ws[2])
            acc = (ws[0] * inv) * o_refs[0][rows, hs].astype(F32)
            acc = acc + (ws[1] * inv) * o_refs[1][rows, hs].astype(F32)
            acc = acc + (ws[2] * inv) * o_refs[2][rows, hs].astype(F32)
            heads.append(acc.astype(BF16))
        return dot(jnp.concatenate(heads, axis=1), wa_ref[...])

    def mixer(rows, ya, yb):
        mix = (jax.nn.sigmoid(ga_ref[rows, :].astype(F32)) * ya
               + jax.nn.sigmoid(gb_ref[rows, :].astype(F32)) * yb)
        return x_ref[rows, :] + dot(mix.astype(BF16), wo_ref[...])

    def ffn_up(h2, lo, hi):
        g = dot(h2, wi_ref[:, lo:hi])
        u = dot(h2, wi_ref[:, FFN_HIDDEN + lo:FFN_HIDDEN + hi])
        return (g * jax.nn.sigmoid(g) * u).astype(BF16)

    yb = [gla_branch(rows) for rows in subs]
    ya = [attn_branch(rows) for rows in subs]
    x1 = [mixer(rows, a, b) for rows, a, b in zip(subs, ya, yb)]
    h2 = [_rms(v, g2_ref[...]).astype(BF16) for v in x1]
    acc = x1
    for lo, hi in chunks:
        act = [ffn_up(h, lo, hi) for h in h2]
        acc = [v + dot(a, wd_ref[lo:hi, :]) for v, a in zip(acc, act)]
    for rows, v in zip(subs, acc):
        out_ref[rows, :] = _rms(v, gf_ref[...])


def _post(x2d, attn_o, attn_st, gla_out, proj, wa, wb, wo, g2, wi, wd, gf, tm, sub_rows, ffn_bounds):
    t = x2d.shape[0]
    resident = lambda shape: pl.BlockSpec(shape, lambda i: (0, 0), pipeline_mode=pl.Buffered(1))
    rows = lambda width, col=0: pl.BlockSpec((tm, width), lambda i: (i, col))
    return pl.pallas_call(
        functools.partial(_post_kernel, sub_rows=sub_rows, ffn_bounds=ffn_bounds),
        grid=(t // tm,),
        in_specs=[
            rows(D_MODEL),
            rows(GROUP_WIDTH), rows(GROUP_WIDTH), rows(GROUP_WIDTH),
            rows(LANES), rows(LANES), rows(LANES),
            rows(D_MODEL),
            rows(D_MODEL, CT_GA // 2), rows(D_MODEL, CT_GB // 2),
            resident(wa.shape), resident(wb.shape), resident(wo.shape), resident(g2.shape),
            resident(wi.shape), resident(wd.shape), resident(gf.shape),
        ],
        out_specs=rows(D_MODEL),
        out_shape=jax.ShapeDtypeStruct((t, D_MODEL), F32),
        compiler_params=_params(("parallel",)),
        name="post",
    )(x2d, *attn_o, *attn_st, gla_out, proj, proj, wa, wb, wo, g2, wi, wd, gf)


def _rope_tables(seq):
    half = ROPE_DIM // 2
    inv_freq = ROPE_THETA ** (-jnp.arange(0, ROPE_DIM, 2, dtype=F32) / ROPE_DIM)
    ang = jnp.arange(seq, dtype=F32)[:, None] * inv_freq[None, :]
    ang = jnp.concatenate([ang, ang], axis=-1)
    cos, sin = jnp.cos(ang), jnp.sin(ang)
    cos_t = jnp.concatenate([cos, jnp.ones((seq, HEAD_DIM - ROPE_DIM), F32)], axis=-1)
    sina_t = jnp.concatenate([-sin[:, :half], jnp.zeros((seq, HEAD_DIM - half), F32)], axis=-1)
    sinb_t = jnp.concatenate([jnp.zeros((seq, half), F32), sin[:, half:],
                              jnp.zeros((seq, HEAD_DIM - ROPE_DIM), F32)], axis=-1)
    return cos_t, sina_t, sinb_t


def _block_tril(n, blk):
    r = np.arange(n)
    return jnp.asarray((r[:, None] // blk == r[None, :] // blk) & (r[None, :] <= r[:, None]), BF16)


def _layer(x2d, batch, seq, norm1_g, w_in, w_gate_lr, b_gate, gla_norm_g, w_branch_a, w_branch_b,
           w_out, norm2_g, w_ffn_in, w_ffn_down, out_g, tables):
    a_w = 3 * GROUP_WIDTH
    gk_w = GLA_HEADS * GLA_KEY_DIM
    gv_w = GLA_HEADS * GLA_VAL_DIM
    o_aq, o_ak, o_av = 0, a_w, 2 * a_w
    o_gq = 3 * a_w
    o_gk = o_gq + gk_w
    o_gv = o_gk + gk_w
    o_gr = o_gv + gv_w
    o_glr = o_gr + gv_w
    o_ga = o_glr + GLA_GATE_RANK
    o_gb = o_ga + D_MODEL
    cols = lambda o, w: w_in[:, o:o + w]
    qkv = lambda g: [cols(o + g * GROUP_WIDTH, GROUP_WIDTH) for o in (o_aq, o_ak, o_av)]
    w_main = jnp.concatenate(
        [cols(o_gv, gv_w), cols(o_gr, gv_w), cols(o_ga, D_MODEL), cols(o_gb, D_MODEL)]
        + qkv(0) + [cols(o_gq, gk_w), cols(o_gk, gk_w)] + qkv(1) + qkv(2), axis=1).astype(BF16)
    w_glr = jnp.pad(cols(o_glr, GLA_GATE_RANK), ((0, 0), (0, LANES - GLA_GATE_RANK))).astype(BF16)
    wg = jnp.pad(w_gate_lr, ((0, LANES - GLA_GATE_RANK), (0, 0))).astype(BF16)

    proj, glr, qkv2, qkv3 = _inproj(x2d, norm1_g[None, :], w_main, w_glr, tables,
                                    batch=batch, seq=seq, tm=512, sub_rows=256)

    qkv1 = proj.reshape(batch, 1, seq, MAIN_WIDTH)
    attn = [
        _attention_group(qkv1, (CT_Q1, CT_K1, CT_V1), 1, batch, seq, tile_positions=512),
        _attention_group(qkv2, (0, 1, 2), 4, batch, seq, tile_positions=2048),
        _attention_group(qkv3, (0, 1, 2), 16, batch, seq, tile_positions=2048),
    ]
    attn_o = [o for o, _ in attn]
    attn_st = [st for _, st in attn]

    gla_out = _gla(proj, glr, wg, b_gate[None, :], gla_norm_g[None, :],
                   batch, seq, tc=1024, blk_rows=256, chunk=128)

    return _post(x2d, attn_o, attn_st, gla_out, proj,
                 w_branch_a.astype(BF16), w_branch_b.astype(BF16), w_out.astype(BF16),
                 norm2_g[None, :], w_ffn_in.astype(BF16), w_ffn_down.astype(BF16), out_g[None, :],
                 tm=512, sub_rows=256, ffn_bounds=(0, 1536, FFN_HIDDEN))


def kernel(x, norm1_g, w_in, w_gate_lr, b_gate, gla_norm_g, w_branch_a, w_branch_b, w_out, norm2_g,
           w_ffn_in, w_ffn_down, norm_f_g):
    batch, seq, d = x.shape
    depth = w_in.shape[0]
    assert depth == 1 and d == D_MODEL
    tables = _rope_tables(seq)
    x2d = x.reshape(batch * seq, d)
    out = _layer(x2d, batch, seq, norm1_g[0], w_in[0], w_gate_lr[0], b_gate[0], gla_norm_g[0],
                 w_branch_a[0], w_branch_b[0], w_out[0], norm2_g[0], w_ffn_in[0], w_ffn_down[0],
                 norm_f_g, tables)
    return out.reshape(batch, seq, d)
```

```python
import functools

import jax
import jax.numpy as jnp
import numpy as np
from jax import lax
from jax.experimental import pallas as pl
from jax.experimental.pallas import tpu as pltpu

F32 = jnp.float32
BF16 = jnp.bfloat16

D_MODEL = 1024
ATTN_GROUPS = ((128, 1), (512, 4), (2048, 16))
HEADS_PER_GROUP = 4
HEAD_DIM = 128
GROUP_WIDTH = HEADS_PER_GROUP * HEAD_DIM
KEYS_BACK = 128
BLOCK_HEADS_PER_TRIP = 32
ROPE_THETA = 500000.0
ROPE_DIM = HEAD_DIM // 4
GLA_HEADS = 4
GLA_KEY_DIM = 128
assert GLA_KEY_DIM == HEAD_DIM
QUERY_SCALE = HEAD_DIM ** -0.5
GLA_VAL_DIM = 256
GLA_GATE_RANK = 16
GLA_GATE_NORMALIZER = 16.0
GLA_SUB = 64
FFN_HIDDEN = 2816
NORM_EPS = 1e-6

LANES = 128
VMEM_LIMIT_BYTES = 56 * 1024 * 1024

COL_TILE = 512
CT_GV, CT_GR, CT_GA, CT_GB = 0, 2, 4, 6
CT_Q1, CT_K1, CT_V1 = 8, 9, 10
CT_GQ, CT_GK = 11, 12
N_MAIN_TILES = 13
MAIN_WIDTH = N_MAIN_TILES * COL_TILE
CT_G2, CT_G3 = 13, 16
N_COL_TILES = 19
QKV_WIDTH = 3 * COL_TILE


def _params(semantics):
    return pltpu.CompilerParams(dimension_semantics=semantics, vmem_limit_bytes=VMEM_LIMIT_BYTES)


def _inproj_kernel(x_ref, g_ref, w_ref, wglr_ref, t1_ref, t4_ref, t16_ref,
                   out_ref, glr_ref, d2_ref, d3_ref, h_sc, *, sub_rows):
    dot = functools.partial(jnp.dot, preferred_element_type=F32)
    tm = x_ref.shape[0]
    x = x_ref[...]
    ms = jnp.mean(x * x, axis=-1, keepdims=True)
    h = (x * lax.rsqrt(ms + NORM_EPS) * g_ref[...]).astype(BF16)
    h_sc[0] = h
    glr_ref[...] = dot(h, wglr_ref[...]).astype(BF16)
    for r0 in range(0, tm, sub_rows):
        rows = slice(r0, r0 + sub_rows)
        for order, dil in ((1, 4), (2, 16)):
            blk = h[rows, :].astype(F32).reshape(sub_rows // dil, dil, D_MODEL)
            h_sc[order, rows, :] = jnp.swapaxes(blk, 0, 1).reshape(sub_rows, D_MODEL).astype(BF16)

    plans = ((CT_G3, 3, 16, d3_ref, 2, t16_ref, 0), (CT_G2, 3, 4, d2_ref, 1, t4_ref, 0),
             (CT_Q1, 3, 1, out_ref, 0, t1_ref, 0), (CT_GQ, 2, 1, out_ref, 0, None, 0),
             (0, CT_Q1, 1, out_ref, 0, None, None))
    for first, count, dilation, dst_ref, order, tab_ref, q_tile in plans:
        per = sub_rows // dilation
        for jt in range(count):
            j = first + jt
            acc = dot(h_sc[order], w_ref[:, j * COL_TILE:(j + 1) * COL_TILE])
            for hh in range(COL_TILE // HEAD_DIM):
                a = acc[:, hh * HEAD_DIM:(hh + 1) * HEAD_DIM]
                if tab_ref is not None and jt < 2:
                    up = pltpu.roll(a, HEAD_DIM - ROPE_DIM // 2, 1)
                    dn = pltpu.roll(a, ROPE_DIM // 2, 1)
                    a = a * tab_ref[0] + up * tab_ref[1] + dn * tab_ref[2]
                if jt == q_tile:
                    a = a * QUERY_SCALE
                a = a.astype(BF16)
                if dilation == 1:
                    lo = j * COL_TILE + hh * HEAD_DIM
                    dst_ref[:, lo:lo + HEAD_DIM] = a
                else:
                    lo = jt * COL_TILE + hh * HEAD_DIM
                    for s in range(tm // sub_rows):
                        for c in range(dilation):
                            src = s * sub_rows + c * per
                            dst_ref[0, c, s * per:(s + 1) * per, lo:lo + HEAD_DIM] = a[src:src + per, :]


def _phase_major_rows(table, sub_rows, dilation):
    seq, lanes = table.shape
    t = table.reshape(seq // sub_rows, sub_rows // dilation, dilation, lanes)
    return t.transpose(0, 2, 1, 3).reshape(seq, lanes)


def _inproj(x2d, g1, w_main, w_glr, tables, batch, seq, tm, sub_rows):
    t = x2d.shape[0]
    pos_blocks = seq // tm
    resident = lambda shape: pl.BlockSpec(shape, lambda i: (0, 0), pipeline_mode=pl.Buffered(1))
    table_spec = pl.BlockSpec((3, tm, LANES), lambda i: (0, i % pos_blocks, 0))
    tab1 = jnp.stack(tables)
    tab4 = jnp.stack([_phase_major_rows(tb, sub_rows, 4) for tb in tables])
    tab16 = jnp.stack([_phase_major_rows(tb, sub_rows, 16) for tb in tables])

    def dilated_spec(dilation):
        return pl.BlockSpec((1, dilation, tm // dilation, QKV_WIDTH),
                            lambda i: (i // pos_blocks, 0, i % pos_blocks, 0))

    return pl.pallas_call(
        functools.partial(_inproj_kernel, sub_rows=sub_rows),
        grid=(t // tm,),
        in_specs=[
            pl.BlockSpec((tm, D_MODEL), lambda i: (i, 0)),
            resident((1, D_MODEL)),
            resident(w_main.shape),
            resident(w_glr.shape),
            table_spec, table_spec, table_spec,
        ],
        out_specs=[
            pl.BlockSpec((tm, MAIN_WIDTH), lambda i: (i, 0)),
            pl.BlockSpec((tm, LANES), lambda i: (i, 0)),
            dilated_spec(4),
            dilated_spec(16),
        ],
        out_shape=[
            jax.ShapeDtypeStruct((t, MAIN_WIDTH), BF16),
            jax.ShapeDtypeStruct((t, LANES), BF16),
            jax.ShapeDtypeStruct((batch, 4, seq // 4, QKV_WIDTH), BF16),
            jax.ShapeDtypeStruct((batch, 16, seq // 16, QKV_WIDTH), BF16),
        ],
        scratch_shapes=[pltpu.VMEM((3, tm, D_MODEL), BF16)],
        compiler_params=_params(("parallel",)),
        name="inproj",
    )(x2d, g1, w_main, w_glr, tab1, tab4, tab16)


def _attn_kernel(q_ref, kp_ref, kc_ref, vp_ref, vc_ref, o_ref, st_ref, *scratch, tq, dilation):
    n = pl.program_id(1)
    blk = KEYS_BACK
    qi = lax.broadcasted_iota(jnp.int32, (blk, 2 * blk), 0)
    jj = lax.broadcasted_iota(jnp.int32, (blk, 2 * blk), 1)
    band = jnp.logical_and(jj >= qi, jj <= qi + blk)
    first_band = jnp.logical_and(band, jj + n * tq >= blk)
    lane = lax.broadcasted_iota(jnp.int32, (blk, LANES), 1)
    neg_inf = jnp.float32(-jnp.inf)

    def windows(c, qb, cur_ref, prev_ref, hs):
        if qb == 0:
            return jnp.concatenate([prev_ref[0, c, :, hs], cur_ref[0, c, 0:blk, hs]], axis=0)
        return cur_ref[0, c, (qb - 1) * blk:(qb + 1) * blk, hs]

    def scores(c, qb):
        valid = first_band if qb == 0 else band
        out = []
        for h in range(HEADS_PER_GROUP):
            hs = slice(h * HEAD_DIM, (h + 1) * HEAD_DIM)
            q = q_ref[0, c, qb * blk:(qb + 1) * blk, hs]
            kk = windows(c, qb, kc_ref, kp_ref, hs)
            s = lax.dot_general(q, kk, (((1,), (1,)), ((), ())), preferred_element_type=F32)
            out.append(jnp.where(valid, s, neg_inf))
        return out

    def finish(c, qb, s_heads):
        rows = slice(qb * blk, (qb + 1) * blk)
        nat = pl.ds(c + qb * blk * dilation, blk, stride=dilation)
        stats = jnp.zeros((blk, LANES), F32)
        for h, s in enumerate(s_heads):
            hs = slice(h * HEAD_DIM, (h + 1) * HEAD_DIM)
            vv = windows(c, qb, vc_ref, vp_ref, hs)
            m = jnp.max(s, axis=-1, keepdims=True)
            p = jnp.exp(s - m)
            l = jnp.sum(p, axis=-1, keepdims=True)
            o = jnp.dot(p.astype(BF16), vv, preferred_element_type=F32) / l
            if dilation == 1:
                o_ref[0, rows, hs] = o.astype(BF16)
            else:
                scratch[0][h, nat, :] = o
            stats = jnp.where(lane == h, m, stats)
            stats = jnp.where(lane == HEADS_PER_GROUP + h, l, stats)
        if dilation == 1:
            st_ref[0, rows, :] = stats
        else:
            st_ref[0, nat, :] = stats

    def run(blocks):
        pending = None
        for c, qb in blocks:
            s_heads = scores(c, qb)
            if pending is not None:
                finish(*pending)
            pending = (c, qb, s_heads)
        finish(*pending)

    if dilation == 1:
        run([(0, qb) for qb in range(tq // blk)])
    else:
        per_trip = max(1, BLOCK_HEADS_PER_TRIP // (HEADS_PER_GROUP * (tq // blk)))

        def body(t, carry):
            run([(t * per_trip + u, qb) for u in range(per_trip) for qb in range(tq // blk)])
            return carry
        lax.fori_loop(0, dilation // per_trip, body, 0)
        for h in range(HEADS_PER_GROUP):
            o_ref[0, :, h * HEAD_DIM:(h + 1) * HEAD_DIM] = scratch[0][h].astype(BF16)


def _attention_group(qkv, col_tiles, dilation, batch, seq, tile_positions):
    r = dilation
    tq = tile_positions // r
    nblk = seq // tile_positions
    prev_per_blk = tq // KEYS_BACK
    cq, ck, cv = col_tiles

    def cur(ct):
        return pl.BlockSpec((1, r, tq, GROUP_WIDTH), lambda b, n: (b, 0, n, ct))

    def prev(ct):
        return pl.BlockSpec((1, r, KEYS_BACK, GROUP_WIDTH),
                            lambda b, n: (b, 0, jnp.maximum(n * prev_per_blk - 1, 0), ct))

    scratch = []
    if r > 1:
        scratch.append(pltpu.VMEM((HEADS_PER_GROUP, tile_positions, HEAD_DIM), F32))
    o, st = pl.pallas_call(
        functools.partial(_attn_kernel, tq=tq, dilation=r),
        grid=(batch, nblk),
        in_specs=[cur(cq), prev(ck), cur(ck), prev(cv), cur(cv)],
        out_specs=[
            pl.BlockSpec((1, tile_positions, GROUP_WIDTH), lambda b, n: (b, n, 0)),
            pl.BlockSpec((1, tile_positions, LANES), lambda b, n: (b, n, 0)),
        ],
        out_shape=[
            jax.ShapeDtypeStruct((batch, seq, GROUP_WIDTH), BF16),
            jax.ShapeDtypeStruct((batch, seq, LANES), F32),
        ],
        scratch_shapes=scratch,
        compiler_params=_params(("parallel", "arbitrary")),
        name=f"attn_r{r}",
    )(qkv, qkv, qkv, qkv, qkv)
    return o.reshape(batch * seq, GROUP_WIDTH), st.reshape(batch * seq, LANES)


def _gla_kernel(q_ref, k_ref, v_ref, gr_ref, glr_ref, wg_ref, bg_ref, gn_ref, tri_ref,
                o_ref, state_sc, *, blk_rows, chunk):
    n = pl.program_id(1)

    @pl.when(n == 0)
    def _():
        state_sc[...] = jnp.zeros_like(state_sc)

    sub = GLA_SUB
    nsub = chunk // sub
    sub_shift = sub.bit_length() - 1
    row = lax.broadcasted_iota(jnp.int32, (chunk, chunk), 0)
    colm = lax.broadcasted_iota(jnp.int32, (chunk, chunk), 1)
    diag_mask = jnp.logical_and((row >> sub_shift) == (colm >> sub_shift), colm <= row)
    dn_t = (((1,), (1,)), ((), ()))
    dn_l = (((0,), (0,)), ((), ()))
    subs = [slice(sb * sub, (sb + 1) * sub) for sb in range(nsub)]
    rows_cat = functools.partial(jnp.concatenate, axis=0)

    def chunk_head(r0, bcum, c, h):
        rows = pl.ds(pl.multiple_of(r0 + c * chunk, chunk), chunk)
        ks = slice(h * GLA_KEY_DIM, (h + 1) * GLA_KEY_DIM)
        vs = slice(h * GLA_VAL_DIM, (h + 1) * GLA_VAL_DIM)
        b = bcum[c * chunk:(c + 1) * chunk, ks]
        q = q_ref[rows, ks].astype(F32)
        k = k_ref[rows, ks].astype(F32)
        v = v_ref[rows, vs]
        cen = [b[sb * sub + sub // 2:sb * sub + sub // 2 + 1, :] for sb in range(nsub)]
        bnd = [None] + [b[sb * sub - 1:sb * sub, :] for sb in range(1, nsub)]
        b_last = b[chunk - 1:chunk, :]
        dev = [b[s, :] - cen[sb] for sb, s in enumerate(subs)]
        q_c = [q[s, :] * jnp.exp(dev[sb]) for sb, s in enumerate(subs)]
        k_c = [k[s, :] * jnp.exp(-dev[sb]) for sb, s in enumerate(subs)]
        a = jnp.where(diag_mask,
                      lax.dot_general(rows_cat(q_c).astype(BF16), rows_cat(k_c).astype(BF16), dn_t,
                                      preferred_element_type=F32), 0.0)
        for sb in range(1, nsub):
            qb = (q_c[sb] * jnp.exp(cen[sb] - bnd[sb])).astype(BF16)
            kb = [k_c[t] * jnp.exp(bnd[sb] - cen[t]) for t in range(sb)]
            kb.append(jnp.zeros((chunk - sb * sub, GLA_KEY_DIM), F32))
            off = lax.dot_general(qb, rows_cat(kb).astype(BF16), dn_t, preferred_element_type=F32)
            pieces = [jnp.zeros((sb * sub, chunk), F32), off]
            if chunk - (sb + 1) * sub:
                pieces.append(jnp.zeros((chunk - (sb + 1) * sub, chunk), F32))
            a = a + rows_cat(pieces)
        q_in = rows_cat([q_c[sb] * jnp.exp(cen[sb]) for sb in range(nsub)]).astype(BF16)
        k_st = rows_cat([k_c[sb] * jnp.exp(b_last - cen[sb]) for sb in range(nsub)]).astype(BF16)
        st = state_sc[h]
        o_inter = lax.dot_general(q_in, st.astype(BF16), dn_t, preferred_element_type=F32)
        state_sc[h] = jnp.exp(b_last) * st + lax.dot_general(v, k_st, dn_l,
                                                               preferred_element_type=F32)
        o = o_inter + jnp.dot(a.astype(BF16), v, preferred_element_type=F32)
        ms = jnp.mean(o * o, axis=-1, keepdims=True)
        y = o * lax.rsqrt(ms + NORM_EPS) * gn_ref[...]
        g = gr_ref[rows, vs].astype(F32)
        o_ref[rows, vs] = (y * (g * jax.nn.sigmoid(g))).astype(BF16)

    def block(i, carry):
        r0 = pl.multiple_of(i * blk_rows, blk_rows)
        z = jnp.dot(glr_ref[pl.ds(r0, blk_rows), :], wg_ref[...],
                    preferred_element_type=F32) + bg_ref[...]
        log_a = (jnp.minimum(z, 0.0) - jnp.log(1.0 + jnp.exp(-jnp.abs(z)))) / GLA_GATE_NORMALIZER
        hi = log_a.astype(BF16)
        lo = (log_a - hi.astype(F32)).astype(BF16)
        bcum = (jnp.dot(tri_ref[...], hi, preferred_element_type=F32)
                + jnp.dot(tri_ref[...], lo, preferred_element_type=F32))
        for c in range(blk_rows // chunk):
            for h in range(GLA_HEADS):
                chunk_head(r0, bcum, c, h)
        return carry

    lax.fori_loop(0, q_ref.shape[0] // blk_rows, block, 0)


def _gla(proj, glr, wg, bg, gn, batch, seq, tc, blk_rows, chunk):
    tri = _block_tril(blk_rows, chunk)
    nblk = seq // tc
    row = lambda b, n: b * nblk + n
    return pl.pallas_call(
        functools.partial(_gla_kernel, blk_rows=blk_rows, chunk=chunk),
        grid=(batch, nblk),
        in_specs=[
            pl.BlockSpec((tc, COL_TILE), lambda b, n: (row(b, n), CT_GQ)),
            pl.BlockSpec((tc, COL_TILE), lambda b, n: (row(b, n), CT_GK)),
            pl.BlockSpec((tc, 2 * COL_TILE), lambda b, n: (row(b, n), CT_GV // 2)),
            pl.BlockSpec((tc, 2 * COL_TILE), lambda b, n: (row(b, n), CT_GR // 2)),
            pl.BlockSpec((tc, LANES), lambda b, n: (row(b, n), 0)),
            pl.BlockSpec((LANES, GLA_HEADS * GLA_KEY_DIM), lambda b, n: (0, 0)),
            pl.BlockSpec((1, GLA_HEADS * GLA_KEY_DIM), lambda b, n: (0, 0)),
            pl.BlockSpec((1, GLA_VAL_DIM), lambda b, n: (0, 0)),
            pl.BlockSpec((blk_rows, blk_rows), lambda b, n: (0, 0)),
        ],
        out_specs=pl.BlockSpec((tc, GLA_HEADS * GLA_VAL_DIM), lambda b, n: (row(b, n), 0)),
        out_shape=jax.ShapeDtypeStruct((batch * seq, GLA_HEADS * GLA_VAL_DIM), BF16),
        scratch_shapes=[pltpu.VMEM((GLA_HEADS, GLA_VAL_DIM, GLA_KEY_DIM), F32)],
        compiler_params=_params(("parallel", "arbitrary")),
        name="gla",
    )(proj, proj, proj, proj, glr, wg, bg, gn, tri)


def _rms(x, g):
    ms = jnp.mean(x * x, axis=-1, keepdims=True)
    return x * lax.rsqrt(ms + NORM_EPS) * g


def _post_kernel(x_ref, o1_ref, o2_ref, o3_ref, s1_ref, s2_ref, s3_ref, gla_ref, ga_ref, gb_ref,
                 wa_ref, wb_ref, wo_ref, g2_ref, wi_ref, wd_ref, gf_ref, out_ref, *, sub_rows,
                 ffn_bounds):
    o_refs = (o1_ref, o2_ref, o3_ref)
    s_refs = (s1_ref, s2_ref, s3_ref)
    subs = [slice(r0, r0 + sub_rows) for r0 in range(0, x_ref.shape[0], sub_rows)]
    chunks = list(zip(ffn_bounds[:-1], ffn_bounds[1:]))
    dot = functools.partial(jnp.dot, preferred_element_type=F32)

    def gla_branch(rows):
        return dot(gla_ref[rows, :], wb_ref[...])

    def attn_branch(rows):
        stats = [s[rows, :] for s in s_refs]
        heads = []
        for h in range(HEADS_PER_GROUP):
            hs = slice(h * HEAD_DIM, (h + 1) * HEAD_DIM)
            ms = [s[:, h:h + 1] for s in stats]
            ls = [s[:, HEADS_PER_GROUP + h:HEADS_PER_GROUP + h + 1] for s in stats]
            m_all = jnp.maximum(jnp.maximum(ms[0], ms[1]), ms[2])
            ws = [l * jnp.exp(m - m_all) for m, l in zip(ms, ls)]
            inv = 1.0 / (ws[0] + ws[1] + ws[2])
            acc = (ws[0] * inv) * o_refs[0][rows, hs].astype(F32)
            acc = acc + (ws[1] * inv) * o_refs[1][rows, hs].astype(F32)
            acc = acc + (ws[2] * inv) * o_refs[2][rows, hs].astype(F32)
            heads.append(acc.astype(BF16))
        return dot(jnp.concatenate(heads, axis=1), wa_ref[...])

    def mixer(rows, ya, yb):
        mix = (jax.nn.sigmoid(ga_ref[rows, :].astype(F32)) * ya
               + jax.nn.sigmoid(gb_ref[rows, :].astype(F32)) * yb)
        return x_ref[rows, :] + dot(mix.astype(BF16), wo_ref[...])

    def ffn_up(h2, lo, hi):
        g = dot(h2, wi_ref[:, lo:hi])
        u = dot(h2, wi_ref[:, FFN_HIDDEN + lo:FFN_HIDDEN + hi])
        return (g * jax.nn.sigmoid(g) * u).astype(BF16)

    yb = [gla_branch(rows) for rows in subs]
    ya = [attn_branch(rows) for rows in subs]
    x1 = [mixer(rows, a, b) for rows, a, b in zip(subs, ya, yb)]
    h2 = [_rms(v, g2_ref[...]).astype(BF16) for v in x1]
    acc = x1
    for lo, hi in chunks:
        act = [ffn_up(h, lo, hi) for h in h2]
        acc = [v + dot(a, wd_ref[lo:hi, :]) for v, a in zip(acc, act)]
    for rows, v in zip(subs, acc):
        out_ref[rows, :] = _rms(v, gf_ref[...])


def _post(x2d, attn_o, attn_st, gla_out, proj, wa, wb, wo, g2, wi, wd, gf, tm, sub_rows, ffn_bounds):
    t = x2d.shape[0]
    resident = lambda shape: pl.BlockSpec(shape, lambda i: (0, 0), pipeline_mode=pl.Buffered(1))
    rows = lambda width, col=0: pl.BlockSpec((tm, width), lambda i: (i, col))
    return pl.pallas_call(
        functools.partial(_post_kernel, sub_rows=sub_rows, ffn_bounds=ffn_bounds),
        grid=(t // tm,),
        in_specs=[
            rows(D_MODEL),
            rows(GROUP_WIDTH), rows(GROUP_WIDTH), rows(GROUP_WIDTH),
            rows(LANES), rows(LANES), rows(LANES),
            rows(D_MODEL),
            rows(D_MODEL, CT_GA // 2), rows(D_MODEL, CT_GB // 2),
            resident(wa.shape), resident(wb.shape), resident(wo.shape), resident(g2.shape),
            resident(wi.shape), resident(wd.shape), resident(gf.shape),
        ],
        out_specs=rows(D_MODEL),
        out_shape=jax.ShapeDtypeStruct((t, D_MODEL), F32),
        compiler_params=_params(("parallel",)),
        name="post",
    )(x2d, *attn_o, *attn_st, gla_out, proj, proj, wa, wb, wo, g2, wi, wd, gf)


def _rope_tables(seq):
    half = ROPE_DIM // 2
    inv_freq = ROPE_THETA ** (-jnp.arange(0, ROPE_DIM, 2, dtype=F32) / ROPE_DIM)
    ang = jnp.arange(seq, dtype=F32)[:, None] * inv_freq[None, :]
    ang = jnp.concatenate([ang, ang], axis=-1)
    cos, sin = jnp.cos(ang), jnp.sin(ang)
    cos_t = jnp.concatenate([cos, jnp.ones((seq, HEAD_DIM - ROPE_DIM), F32)], axis=-1)
    sina_t = jnp.concatenate([-sin[:, :half], jnp.zeros((seq, HEAD_DIM - half), F32)], axis=-1)
    sinb_t = jnp.concatenate([jnp.zeros((seq, half), F32), sin[:, half:],
                              jnp.zeros((seq, HEAD_DIM - ROPE_DIM), F32)], axis=-1)
    return cos_t, sina_t, sinb_t


def _block_tril(n, blk):
    r = np.arange(n)
    return jnp.asarray((r[:, None] // blk == r[None, :] // blk) & (r[None, :] <= r[:, None]), BF16)


def _layer(x2d, batch, seq, norm1_g, w_in, w_gate_lr, b_gate, gla_norm_g, w_branch_a, w_branch_b,
           w_out, norm2_g, w_ffn_in, w_ffn_down, out_g, tables):
    a_w = 3 * GROUP_WIDTH
    gk_w = GLA_HEADS * GLA_KEY_DIM
    gv_w = GLA_HEADS * GLA_VAL_DIM
    o_aq, o_ak, o_av = 0, a_w, 2 * a_w
    o_gq = 3 * a_w
    o_gk = o_gq + gk_w
    o_gv = o_gk + gk_w
    o_gr = o_gv + gv_w
    o_glr = o_gr + gv_w
    o_ga = o_glr + GLA_GATE_RANK
    o_gb = o_ga + D_MODEL
    cols = lambda o, w: w_in[:, o:o + w]
    qkv = lambda g: [cols(o + g * GROUP_WIDTH, GROUP_WIDTH) for o in (o_aq, o_ak, o_av)]
    w_main = jnp.concatenate(
        [cols(o_gv, gv_w), cols(o_gr, gv_w), cols(o_ga, D_MODEL), cols(o_gb, D_MODEL)]
        + qkv(0) + [cols(o_gq, gk_w), cols(o_gk, gk_w)] + qkv(1) + qkv(2), axis=1).astype(BF16)
    w_glr = jnp.pad(cols(o_glr, GLA_GATE_RANK), ((0, 0), (0, LANES - GLA_GATE_RANK))).astype(BF16)
    wg = jnp.pad(w_gate_lr, ((0, LANES - GLA_GATE_RANK), (0, 0))).astype(BF16)

    proj, glr, qkv2, qkv3 = _inproj(x2d, norm1_g[None, :], w_main, w_glr, tables,
                                    batch=batch, seq=seq, tm=512, sub_rows=256)

    qkv1 = proj.reshape(batch, 1, seq, MAIN_WIDTH)
    attn = [
        _attention_group(qkv1, (CT_Q1, CT_K1, CT_V1), 1, batch, seq, tile_positions=1024),
        _attention_group(qkv2, (0, 1, 2), 4, batch, seq, tile_positions=2048),
        _attention_group(qkv3, (0, 1, 2), 16, batch, seq, tile_positions=2048),
    ]
    attn_o = [o for o, _ in attn]
    attn_st = [st for _, st in attn]

    gla_out = _gla(proj, glr, wg, b_gate[None, :], gla_norm_g[None, :],
                   batch, seq, tc=2048, blk_rows=256, chunk=128)

    return _post(x2d, attn_o, attn_st, gla_out, proj,
                 w_branch_a.astype(BF16), w_branch_b.astype(BF16), w_out.astype(BF16),
                 norm2_g[None, :], w_ffn_in.astype(BF16), w_ffn_down.astype(BF16), out_g[None, :],
                 tm=512, sub_rows=256, ffn_bounds=(0, 1536, FFN_HIDDEN))


def kernel(x, norm1_g, w_in, w_gate_lr, b_gate, gla_norm_g, w_branch_a, w_branch_b, w_out, norm2_g,
           w_ffn_in, w_ffn_down, norm_f_g):
    batch, seq, d = x.shape
    depth = w_in.shape[0]
    assert depth == 1 and d == D_MODEL
    tables = _rope_tables(seq)
    x2d = x.reshape(batch * seq, d)
    out = _layer(x2d, batch, seq, norm1_g[0], w_in[0], w_gate_lr[0], b_gate[0], gla_norm_g[0],
                 w_branch_a[0], w_branch_b[0], w_out[0], norm2_g[0], w_ffn_in[0], w_ffn_down[0],
                 norm_f_g, tables)
    return out.reshape(batch, seq, d)
```

```python
import functools

import jax
import jax.numpy as jnp
import numpy as np
from jax import lax
from jax.experimental import pallas as pl
from jax.experimental.pallas import tpu as pltpu

F32 = jnp.float32
BF16 = jnp.bfloat16

D_MODEL = 1024
ATTN_GROUPS = ((128, 1), (512, 4), (2048, 16))
HEADS_PER_GROUP = 4
HEAD_DIM = 128
GROUP_WIDTH = HEADS_PER_GROUP * HEAD_DIM
KEYS_BACK = 128
BLOCK_HEADS_PER_TRIP = 32
ROPE_THETA = 500000.0
ROPE_DIM = HEAD_DIM // 4
GLA_HEADS = 4
GLA_KEY_DIM = 128
assert GLA_KEY_DIM == HEAD_DIM
QUERY_SCALE = HEAD_DIM ** -0.5
GLA_VAL_DIM = 256
GLA_GATE_RANK = 16
GLA_GATE_NORMALIZER = 16.0
GLA_SUB = 64
FFN_HIDDEN = 2816
NORM_EPS = 1e-6

LANES = 128
VMEM_LIMIT_BYTES = 56 * 1024 * 1024

COL_TILE = 512
CT_GV, CT_GR, CT_GA, CT_GB = 0, 2, 4, 6
CT_Q1, CT_K1, CT_V1 = 8, 9, 10
CT_GQ, CT_GK = 11, 12
N_MAIN_TILES = 13
MAIN_WIDTH = N_MAIN_TILES * COL_TILE
QKV_WIDTH = 3 * COL_TILE
SRC_AQ, SRC_AK, SRC_AV, SRC_GQ, SRC_GK, SRC_GV, SRC_GR = 0, 3, 6, 9, 10, 11, 13
SRC_GLR_COL = 15 * COL_TILE


def _params(semantics):
    return pltpu.CompilerParams(dimension_semantics=semantics, vmem_limit_bytes=VMEM_LIMIT_BYTES)


def _qkv_tiles(group, first_dst):
    return ((0, SRC_AQ + group, first_dst, True, True), (0, SRC_AK + group, first_dst + 1, True, False),
            (0, SRC_AV + group, first_dst + 2, False, False))


def _plain_tiles(weight, first_src, first_dst, count=2):
    return tuple((weight, first_src + t, first_dst + t, False, False) for t in range(count))


def _inproj_kernel(x_ref, g_ref, w_ref, wgate_ref, wglr_ref, t1_ref, t4_ref, t16_ref,
                   out_ref, glr_ref, d2_ref, d3_ref, h_sc, *, sub_rows):
    dot = functools.partial(jnp.dot, preferred_element_type=F32)
    tm = x_ref.shape[0]
    x = x_ref[...]
    ms = jnp.mean(x * x, axis=-1, keepdims=True)
    h = (x * lax.rsqrt(ms + NORM_EPS) * g_ref[...]).astype(BF16)
    h_sc[0] = h
    glr_ref[...] = dot(h, wglr_ref[...]).astype(BF16)
    for r0 in range(0, tm, sub_rows):
        rows = slice(r0, r0 + sub_rows)
        for order, dil in ((1, 4), (2, 16)):
            blk = h[rows, :].astype(F32).reshape(sub_rows // dil, dil, D_MODEL)
            h_sc[order, rows, :] = jnp.swapaxes(blk, 0, 1).reshape(sub_rows, D_MODEL).astype(BF16)

    w_refs = (w_ref, wgate_ref)
    plans = ((16, d3_ref, 2, t16_ref, _qkv_tiles(2, 0)),
             (4, d2_ref, 1, t4_ref, _qkv_tiles(1, 0)),
             (1, out_ref, 0, t1_ref, _qkv_tiles(0, CT_Q1)),
             (1, out_ref, 0, None, ((0, SRC_GQ, CT_GQ, False, True), (0, SRC_GK, CT_GK, False, False))),
             (1, out_ref, 0, None, _plain_tiles(0, SRC_GV, CT_GV) + _plain_tiles(0, SRC_GR, CT_GR)
              + _plain_tiles(1, 0, CT_GA) + _plain_tiles(1, 2, CT_GB)))
    for dilation, dst_ref, order, tab_ref, tiles in plans:
        per = sub_rows // dilation
        for weight, src_tile, dst_tile, rope, query in tiles:
            acc = dot(h_sc[order], w_refs[weight][:, src_tile * COL_TILE:(src_tile + 1) * COL_TILE])
            for hh in range(COL_TILE // HEAD_DIM):
                a = acc[:, hh * HEAD_DIM:(hh + 1) * HEAD_DIM]
                if rope:
                    up = pltpu.roll(a, HEAD_DIM - ROPE_DIM // 2, 1)
                    dn = pltpu.roll(a, ROPE_DIM // 2, 1)
                    a = a * tab_ref[0] + up * tab_ref[1] + dn * tab_ref[2]
                if query:
                    a = a * QUERY_SCALE
                a = a.astype(BF16)
                lo = dst_tile * COL_TILE + hh * HEAD_DIM
                if dilation == 1:
                    dst_ref[:, lo:lo + HEAD_DIM] = a
                else:
                    for s in range(tm // sub_rows):
                        for c in range(dilation):
                            src = s * sub_rows + c * per
                            dst_ref[0, c, s * per:(s + 1) * per, lo:lo + HEAD_DIM] = a[src:src + per, :]


def _rope_tables(seq, sub_rows, dilation):
    half = ROPE_DIM // 2
    row = lax.broadcasted_iota(jnp.int32, (seq, HEAD_DIM), 0)
    lane = lax.broadcasted_iota(jnp.int32, (seq, HEAD_DIM), 1)
    per = sub_rows // dilation
    within = row % sub_rows
    pos = (row - within) + dilation * (within % per) + within // per
    inv_freq = ROPE_THETA ** (-(2 * (lane % half)).astype(F32) / ROPE_DIM)
    ang = pos.astype(F32) * inv_freq
    cos, sin = jnp.cos(ang), jnp.sin(ang)
    return jnp.stack([jnp.where(lane < ROPE_DIM, cos, 1.0),
                      jnp.where(lane < half, -sin, 0.0),
                      jnp.where(jnp.logical_and(lane >= half, lane < ROPE_DIM), sin, 0.0)])


def _inproj(x2d, g1, w_main, w_gates, w_glr, batch, seq, tm, sub_rows):
    t = x2d.shape[0]
    pos_blocks = seq // tm
    resident = lambda shape: pl.BlockSpec(shape, lambda i: (0, 0), pipeline_mode=pl.Buffered(1))
    table_spec = pl.BlockSpec((3, tm, LANES), lambda i: (0, i % pos_blocks, 0))
    tab1, tab4, tab16 = (_rope_tables(seq, sub_rows, dilation) for dilation in (1, 4, 16))

    def dilated_spec(dilation):
        return pl.BlockSpec((1, dilation, tm // dilation, QKV_WIDTH),
                            lambda i: (i // pos_blocks, 0, i % pos_blocks, 0))

    return pl.pallas_call(
        functools.partial(_inproj_kernel, sub_rows=sub_rows),
        grid=(t // tm,),
        in_specs=[
            pl.BlockSpec((tm, D_MODEL), lambda i: (i, 0)),
            resident((1, D_MODEL)),
            resident(w_main.shape),
            resident(w_gates.shape),
            resident(w_glr.shape),
            table_spec, table_spec, table_spec,
        ],
        out_specs=[
            pl.BlockSpec((tm, MAIN_WIDTH), lambda i: (i, 0)),
            pl.BlockSpec((tm, LANES), lambda i: (i, 0)),
            dilated_spec(4),
            dilated_spec(16),
        ],
        out_shape=[
            jax.ShapeDtypeStruct((t, MAIN_WIDTH), BF16),
            jax.ShapeDtypeStruct((t, LANES), BF16),
            jax.ShapeDtypeStruct((batch, 4, seq // 4, QKV_WIDTH), BF16),
            jax.ShapeDtypeStruct((batch, 16, seq // 16, QKV_WIDTH), BF16),
        ],
        scratch_shapes=[pltpu.VMEM((3, tm, D_MODEL), BF16)],
        compiler_params=_params(("parallel",)),
        name="inproj",
    )(x2d, g1, w_main, w_gates, w_glr, tab1, tab4, tab16)


def _attn_kernel(q_ref, kp_ref, kc_ref, vp_ref, vc_ref, o_ref, st_ref, *scratch, tq, dilation):
    n = pl.program_id(1)
    blk = KEYS_BACK
    qi = lax.broadcasted_iota(jnp.int32, (blk, 2 * blk), 0)
    jj = lax.broadcasted_iota(jnp.int32, (blk, 2 * blk), 1)
    band = jnp.logical_and(jj >= qi, jj <= qi + blk)
    first_band = jnp.logical_and(band, jj + n * tq >= blk)
    lane = lax.broadcasted_iota(jnp.int32, (blk, LANES), 1)
    neg_inf = jnp.float32(-jnp.inf)

    def windows(c, qb, cur_ref, prev_ref, hs):
        if qb == 0:
            return jnp.concatenate([prev_ref[0, c, :, hs], cur_ref[0, c, 0:blk, hs]], axis=0)
        return cur_ref[0, c, (qb - 1) * blk:(qb + 1) * blk, hs]

    def scores(c, qb):
        valid = first_band if qb == 0 else band
        out = []
        for h in range(HEADS_PER_GROUP):
            hs = slice(h * HEAD_DIM, (h + 1) * HEAD_DIM)
            q = q_ref[0, c, qb * blk:(qb + 1) * blk, hs]
            kk = windows(c, qb, kc_ref, kp_ref, hs)
            s = lax.dot_general(q, kk, (((1,), (1,)), ((), ())), preferred_element_type=F32)
            out.append(jnp.where(valid, s, neg_inf))
        return out

    def finish(c, qb, s_heads):
        rows = slice(qb * blk, (qb + 1) * blk)
        nat = pl.ds(c + qb * blk * dilation, blk, stride=dilation)
        stats = jnp.zeros((blk, LANES), F32)
        for h, s in enumerate(s_heads):
            hs = slice(h * HEAD_DIM, (h + 1) * HEAD_DIM)
            vv = windows(c, qb, vc_ref, vp_ref, hs)
            m = jnp.max(s, axis=-1, keepdims=True)
            p = jnp.exp(s - m)
            l = jnp.sum(p, axis=-1, keepdims=True)
            o = jnp.dot(p.astype(BF16), vv, preferred_element_type=F32) / l
            if dilation == 1:
                o_ref[0, rows, hs] = o.astype(BF16)
            else:
                scratch[0][h, nat, :] = o
            stats = jnp.where(lane == h, m, stats)
            stats = jnp.where(lane == HEADS_PER_GROUP + h, l, stats)
        if dilation == 1:
            st_ref[0, rows, :] = stats
        else:
            st_ref[0, nat, :] = stats

    def run(blocks):
        pending = None
        for c, qb in blocks:
            s_heads = scores(c, qb)
            if pending is not None:
                finish(*pending)
            pending = (c, qb, s_heads)
        finish(*pending)

    if dilation == 1:
        run([(0, qb) for qb in range(tq // blk)])
    else:
        per_trip = max(1, BLOCK_HEADS_PER_TRIP // (HEADS_PER_GROUP * (tq // blk)))

        def body(t, carry):
            run([(t * per_trip + u, qb) for u in range(per_trip) for qb in range(tq // blk)])
            return carry
        lax.fori_loop(0, dilation // per_trip, body, 0)
        for h in range(HEADS_PER_GROUP):
            o_ref[0, :, h * HEAD_DIM:(h + 1) * HEAD_DIM] = scratch[0][h].astype(BF16)


def _attention_group(qkv, col_tiles, dilation, batch, seq, tile_positions):
    r = dilation
    tq = tile_positions // r
    nblk = seq // tile_positions
    prev_per_blk = tq // KEYS_BACK
    cq, ck, cv = col_tiles

    def cur(ct):
        return pl.BlockSpec((1, r, tq, GROUP_WIDTH), lambda b, n: (b, 0, n, ct))

    def prev(ct):
        return pl.BlockSpec((1, r, KEYS_BACK, GROUP_WIDTH),
                            lambda b, n: (b, 0, jnp.maximum(n * prev_per_blk - 1, 0), ct))

    scratch = []
    if r > 1:
        scratch.append(pltpu.VMEM((HEADS_PER_GROUP, tile_positions, HEAD_DIM), F32))
    o, st = pl.pallas_call(
        functools.partial(_attn_kernel, tq=tq, dilation=r),
        grid=(batch, nblk),
        in_specs=[cur(cq), prev(ck), cur(ck), prev(cv), cur(cv)],
        out_specs=[
            pl.BlockSpec((1, tile_positions, GROUP_WIDTH), lambda b, n: (b, n, 0)),
            pl.BlockSpec((1, tile_positions, LANES), lambda b, n: (b, n, 0)),
        ],
        out_shape=[
            jax.ShapeDtypeStruct((batch, seq, GROUP_WIDTH), BF16),
            jax.ShapeDtypeStruct((batch, seq, LANES), F32),
        ],
        scratch_shapes=scratch,
        compiler_params=_params(("parallel", "arbitrary")),
        name=f"attn_r{r}",
    )(qkv, qkv, qkv, qkv, qkv)
    return o.reshape(batch * seq, GROUP_WIDTH), st.reshape(batch * seq, LANES)


def _gla_kernel(q_ref, k_ref, v_ref, gr_ref, glr_ref, wg_ref, bg_ref, gn_ref, tri_ref,
                o_ref, state_sc, *, blk_rows, chunk):
    n = pl.program_id(1)

    @pl.when(n == 0)
    def _():
        state_sc[...] = jnp.zeros_like(state_sc)

    sub = GLA_SUB
    nsub = chunk // sub
    sub_shift = sub.bit_length() - 1
    row = lax.broadcasted_iota(jnp.int32, (chunk, chunk), 0)
    colm = lax.broadcasted_iota(jnp.int32, (chunk, chunk), 1)
    diag_mask = jnp.logical_and((row >> sub_shift) == (colm >> sub_shift), colm <= row)
    dn_t = (((1,), (1,)), ((), ()))
    dn_l = (((0,), (0,)), ((), ()))
    subs = [slice(sb * sub, (sb + 1) * sub) for sb in range(nsub)]
    rows_cat = functools.partial(jnp.concatenate, axis=0)

    def chunk_head(r0, bcum, c, h):
        rows = pl.ds(pl.multiple_of(r0 + c * chunk, chunk), chunk)
        ks = slice(h * GLA_KEY_DIM, (h + 1) * GLA_KEY_DIM)
        vs = slice(h * GLA_VAL_DIM, (h + 1) * GLA_VAL_DIM)
        b = bcum[c * chunk:(c + 1) * chunk, ks]
        q = q_ref[rows, ks].astype(F32)
        k = k_ref[rows, ks].astype(F32)
        v = v_ref[rows, vs]
        cen = [b[sb * sub + sub // 2:sb * sub + sub // 2 + 1, :] for sb in range(nsub)]
        bnd = [None] + [b[sb * sub - 1:sb * sub, :] for sb in range(1, nsub)]
        b_last = b[chunk - 1:chunk, :]
        dev = [b[s, :] - cen[sb] for sb, s in enumerate(subs)]
        q_c = [q[s, :] * jnp.exp(dev[sb]) for sb, s in enumerate(subs)]
        k_c = [k[s, :] * jnp.exp(-dev[sb]) for sb, s in enumerate(subs)]
        a = jnp.where(diag_mask,
                      lax.dot_general(rows_cat(q_c).astype(BF16), rows_cat(k_c).astype(BF16), dn_t,
                                      preferred_element_type=F32), 0.0)
        for sb in range(1, nsub):
            qb = (q_c[sb] * jnp.exp(cen[sb] - bnd[sb])).astype(BF16)
            kb = [k_c[t] * jnp.exp(bnd[sb] - cen[t]) for t in range(sb)]
            kb.append(jnp.zeros((chunk - sb * sub, GLA_KEY_DIM), F32))
            off = lax.dot_general(qb, rows_cat(kb).astype(BF16), dn_t, preferred_element_type=F32)
            pieces = [jnp.zeros((sb * sub, chunk), F32), off]
            if chunk - (sb + 1) * sub:
                pieces.append(jnp.zeros((chunk - (sb + 1) * sub, chunk), F32))
            a = a + rows_cat(pieces)
        q_in = rows_cat([q_c[sb] * jnp.exp(cen[sb]) for sb in range(nsub)]).astype(BF16)
        k_st = rows_cat([k_c[sb] * jnp.exp(b_last - cen[sb]) for sb in range(nsub)]).astype(BF16)
        st = state_sc[h]
        o_inter = lax.dot_general(q_in, st.astype(BF16), dn_t, preferred_element_type=F32)
        state_sc[h] = jnp.exp(b_last) * st + lax.dot_general(v, k_st, dn_l,
                                                               preferred_element_type=F32)
        o = o_inter + jnp.dot(a.astype(BF16), v, preferred_element_type=F32)
        ms = jnp.mean(o * o, axis=-1, keepdims=True)
        y = o * lax.rsqrt(ms + NORM_EPS) * gn_ref[...]
        g = gr_ref[rows, vs].astype(F32)
        o_ref[rows, vs] = (y * (g * jax.nn.sigmoid(g))).astype(BF16)

    def block(i, carry):
        r0 = pl.multiple_of(i * blk_rows, blk_rows)
        z = jnp.dot(glr_ref[pl.ds(r0, blk_rows), :], wg_ref[...],
                    preferred_element_type=F32) + bg_ref[...]
        log_a = (jnp.minimum(z, 0.0) - jnp.log(1.0 + jnp.exp(-jnp.abs(z)))) / GLA_GATE_NORMALIZER
        hi = log_a.astype(BF16)
        lo = (log_a - hi.astype(F32)).astype(BF16)
        bcum = (jnp.dot(tri_ref[...], hi, preferred_element_type=F32)
                + jnp.dot(tri_ref[...], lo, preferred_element_type=F32))
        for c in range(blk_rows // chunk):
            for h in range(GLA_HEADS):
                chunk_head(r0, bcum, c, h)
        return carry

    lax.fori_loop(0, q_ref.shape[0] // blk_rows, block, 0)


def _gla(proj, glr, wg, bg, gn, batch, seq, tc, blk_rows, chunk):
    tri = _block_tril(blk_rows, chunk)
    nblk = seq // tc
    row = lambda b, n: b * nblk + n
    return pl.pallas_call(
        functools.partial(_gla_kernel, blk_rows=blk_rows, chunk=chunk),
        grid=(batch, nblk),
        in_specs=[
            pl.BlockSpec((tc, COL_TILE), lambda b, n: (row(b, n), CT_GQ)),
            pl.BlockSpec((tc, COL_TILE), lambda b, n: (row(b, n), CT_GK)),
            pl.BlockSpec((tc, 2 * COL_TILE), lambda b, n: (row(b, n), CT_GV // 2)),
            pl.BlockSpec((tc, 2 * COL_TILE), lambda b, n: (row(b, n), CT_GR // 2)),
            pl.BlockSpec((tc, LANES), lambda b, n: (row(b, n), 0)),
            pl.BlockSpec((LANES, GLA_HEADS * GLA_KEY_DIM), lambda b, n: (0, 0)),
            pl.BlockSpec((1, GLA_HEADS * GLA_KEY_DIM), lambda b, n: (0, 0)),
            pl.BlockSpec((1, GLA_VAL_DIM), lambda b, n: (0, 0)),
            pl.BlockSpec((blk_rows, blk_rows), lambda b, n: (0, 0)),
        ],
        out_specs=pl.BlockSpec((tc, GLA_HEADS * GLA_VAL_DIM), lambda b, n: (row(b, n), 0)),
        out_shape=jax.ShapeDtypeStruct((batch * seq, GLA_HEADS * GLA_VAL_DIM), BF16),
        scratch_shapes=[pltpu.VMEM((GLA_HEADS, GLA_VAL_DIM, GLA_KEY_DIM), F32)],
        compiler_params=_params(("parallel", "arbitrary")),
        name="gla",
    )(proj, proj, proj, proj, glr, wg, bg, gn, tri)


def _rms(x, g):
    ms = jnp.mean(x * x, axis=-1, keepdims=True)
    return x * lax.rsqrt(ms + NORM_EPS) * g


def _post_kernel(x_ref, o1_ref, o2_ref, o3_ref, s1_ref, s2_ref, s3_ref, gla_ref, ga_ref, gb_ref,
                 wa_ref, wb_ref, wo_ref, g2_ref, wi_ref, wd_ref, gf_ref, out_ref, *, sub_rows,
                 ffn_bounds):
    o_refs = (o1_ref, o2_ref, o3_ref)
    s_refs = (s1_ref, s2_ref, s3_ref)
    subs = [slice(r0, r0 + sub_rows) for r0 in range(0, x_ref.shape[0], sub_rows)]
    chunks = list(zip(ffn_bounds[:-1], ffn_bounds[1:]))
    dot = functools.partial(jnp.dot, preferred_element_type=F32)

    def gla_branch(rows):
        return dot(gla_ref[rows, :], wb_ref[...])

    def attn_branch(rows):
        stats = [s[rows, :] for s in s_refs]
        heads = []
        for h in range(HEADS_PER_GROUP):
            hs = slice(h * HEAD_DIM, (h + 1) * HEAD_DIM)
            ms = [s[:, h:h + 1] for s in stats]
            ls = [s[:, HEADS_PER_GROUP + h:HEADS_PER_GROUP + h + 1] for s in stats]
            m_all = jnp.maximum(jnp.maximum(ms[0], ms[1]), ms[2])
            ws = [l * jnp.exp(m - m_all) for m, l in zip(ms, ls)]
            inv = 1.0 / (ws[0] + ws[1] + ws[2])
            acc = (ws[0] * inv) * o_refs[0][rows, hs].astype(F32)
            acc = acc + (ws[1] * inv) * o_refs[1][rows, hs].astype(F32)
            acc = acc + (ws[2] * inv) * o_refs[2][rows, hs].astype(F32)
            heads.append(acc.astype(BF16))
        return dot(jnp.concatenate(heads, axis=1), wa_ref[...])

    def mixer(rows, ya, yb):
        mix = (jax.nn.sigmoid(ga_ref[rows, :].astype(F32)) * ya
               + jax.nn.sigmoid(gb_ref[rows, :].astype(F32)) * yb)
        return x_ref[rows, :] + dot(mix.astype(BF16), wo_ref[...])

    def ffn_up(h2, lo, hi):
        g = dot(h2, wi_ref[:, lo:hi])
        u = dot(h2, wi_ref[:, FFN_HIDDEN + lo:FFN_HIDDEN + hi])
        return (g * jax.nn.sigmoid(g) * u).astype(BF16)

    yb = [gla_branch(rows) for rows in subs]
    ya = [attn_branch(rows) for rows in subs]
    x1 = [mixer(rows, a, b) for rows, a, b in zip(subs, ya, yb)]
    h2 = [_rms(v, g2_ref[...]).astype(BF16) for v in x1]
    acc = x1
    for lo, hi in chunks:
        act = [ffn_up(h, lo, hi) for h in h2]
        acc = [v + dot(a, wd_ref[lo:hi, :]) for v, a in zip(acc, act)]
    for rows, v in zip(subs, acc):
        out_ref[rows, :] = _rms(v, gf_ref[...])


def _post(x2d, attn_o, attn_st, gla_out, proj, wa, wb, wo, g2, wi, wd, gf, tm, sub_rows, ffn_bounds):
    t = x2d.shape[0]
    resident = lambda shape: pl.BlockSpec(shape, lambda i: (0, 0), pipeline_mode=pl.Buffered(1))
    rows = lambda width, col=0: pl.BlockSpec((tm, width), lambda i: (i, col))
    return pl.pallas_call(
        functools.partial(_post_kernel, sub_rows=sub_rows, ffn_bounds=ffn_bounds),
        grid=(t // tm,),
        in_specs=[
            rows(D_MODEL),
            rows(GROUP_WIDTH), rows(GROUP_WIDTH), rows(GROUP_WIDTH),
            rows(LANES), rows(LANES), rows(LANES),
            rows(D_MODEL),
            rows(D_MODEL, CT_GA // 2), rows(D_MODEL, CT_GB // 2),
            resident(wa.shape), resident(wb.shape), resident(wo.shape), resident(g2.shape),
            resident(wi.shape), resident(wd.shape), resident(gf.shape),
        ],
        out_specs=rows(D_MODEL),
        out_shape=jax.ShapeDtypeStruct((t, D_MODEL), F32),
        compiler_params=_params(("parallel",)),
        name="post",
    )(x2d, *attn_o, *attn_st, gla_out, proj, proj, wa, wb, wo, g2, wi, wd, gf)


def _block_tril(n, blk):
    r = np.arange(n)
    return jnp.asarray((r[:, None] // blk == r[None, :] // blk) & (r[None, :] <= r[:, None]), BF16)


def _layer(x2d, batch, seq, norm1_g, w_in, w_gate_lr, b_gate, gla_norm_g, w_branch_a, w_branch_b,
           w_out, norm2_g, w_ffn_in, w_ffn_down, out_g):
    assert w_in.shape[1] == SRC_GLR_COL + GLA_GATE_RANK + 2 * D_MODEL
    w_main = w_in[:, :SRC_GLR_COL].astype(BF16)
    w_glr = jnp.pad(w_in[:, SRC_GLR_COL:SRC_GLR_COL + GLA_GATE_RANK],
                    ((0, 0), (0, LANES - GLA_GATE_RANK))).astype(BF16)
    w_gates = w_in[:, SRC_GLR_COL + GLA_GATE_RANK:].astype(BF16)
    wg = jnp.pad(w_gate_lr, ((0, LANES - GLA_GATE_RANK), (0, 0))).astype(BF16)

    proj, glr, qkv2, qkv3 = _inproj(x2d, norm1_g[None, :], w_main, w_gates, w_glr,
                                    batch=batch, seq=seq, tm=512, sub_rows=256)

    qkv1 = proj.reshape(batch, 1, seq, MAIN_WIDTH)
    attn = [
        _attention_group(qkv1, (CT_Q1, CT_K1, CT_V1), 1, batch, seq, tile_positions=1024),
        _attention_group(qkv2, (0, 1, 2), 4, batch, seq, tile_positions=2048),
        _attention_group(qkv3, (0, 1, 2), 16, batch, seq, tile_positions=2048),
    ]
    attn_o = [o for o, _ in attn]
    attn_st = [st for _, st in attn]

    gla_out = _gla(proj, glr, wg, b_gate[None, :], gla_norm_g[None, :],
                   batch, seq, tc=2048, blk_rows=256, chunk=128)

    return _post(x2d, attn_o, attn_st, gla_out, proj,
                 w_branch_a.astype(BF16), w_branch_b.astype(BF16), w_out.astype(BF16),
                 norm2_g[None, :], w_ffn_in.astype(BF16), w_ffn_down.astype(BF16), out_g[None, :],
                 tm=512, sub_rows=256, ffn_bounds=(0, 1536, FFN_HIDDEN))


def kernel(x, norm1_g, w_in, w_gate_lr, b_gate, gla_norm_g, w_branch_a, w_branch_b, w_out, norm2_g,
           w_ffn_in, w_ffn_down, norm_f_g):
    batch, seq, d = x.shape
    depth = w_in.shape[0]
    assert depth == 1 and d == D_MODEL
    x2d = x.reshape(batch * seq, d)
    out = _layer(x2d, batch, seq, norm1_g[0], w_in[0], w_gate_lr[0], b_gate[0], gla_norm_g[0],
                 w_branch_a[0], w_branch_b[0], w_out[0], norm2_g[0], w_ffn_in[0], w_ffn_down[0],
                 norm_f_g)
    return out.reshape(batch, seq, d)
```

```python
import functools

import jax
import jax.numpy as jnp
import numpy as np
from jax import lax
from jax.experimental import pallas as pl
from jax.experimental.pallas import tpu as pltpu

F32 = jnp.float32
BF16 = jnp.bfloat16

D_MODEL = 1024
ATTN_GROUPS = ((128, 1), (512, 4), (2048, 16))
HEADS_PER_GROUP = 4
HEAD_DIM = 128
GROUP_WIDTH = HEADS_PER_GROUP * HEAD_DIM
KEYS_BACK = 128
BLOCK_HEADS_PER_TRIP = 64
ROPE_THETA = 500000.0
ROPE_DIM = HEAD_DIM // 4
GLA_HEADS = 4
GLA_KEY_DIM = 128
assert GLA_KEY_DIM == HEAD_DIM
QUERY_SCALE = HEAD_DIM ** -0.5
GLA_VAL_DIM = 256
GLA_GATE_RANK = 16
GLA_GATE_NORMALIZER = 16.0
GLA_SUB = 64
FFN_HIDDEN = 2816
NORM_EPS = 1e-6

LANES = 128
VMEM_LIMIT_BYTES = 56 * 1024 * 1024

COL_TILE = 512
CT_GV, CT_GR, CT_GA, CT_GB = 0, 2, 4, 6
CT_Q1, CT_K1, CT_V1 = 8, 9, 10
CT_GQ, CT_GK = 11, 12
N_MAIN_TILES = 13
MAIN_WIDTH = N_MAIN_TILES * COL_TILE
QKV_WIDTH = 3 * COL_TILE
SRC_AQ, SRC_AK, SRC_AV, SRC_GQ, SRC_GK, SRC_GV, SRC_GR = 0, 3, 6, 9, 10, 11, 13
SRC_GLR_COL = 15 * COL_TILE


def _params(semantics):
    return pltpu.CompilerParams(dimension_semantics=semantics, vmem_limit_bytes=VMEM_LIMIT_BYTES)


def _qkv_tiles(group, first_dst):
    return ((0, SRC_AQ + group, first_dst, True, True), (0, SRC_AK + group, first_dst + 1, True, False),
            (0, SRC_AV + group, first_dst + 2, False, False))


def _plain_tiles(weight, first_src, first_dst, count=2):
    return tuple((weight, first_src + t, first_dst + t, False, False) for t in range(count))


def _inproj_kernel(x_ref, g_ref, w_ref, wgate_ref, wglr_ref, t1_ref, t4_ref, t16_ref,
                   out_ref, glr_ref, d2_ref, d3_ref, h_sc, *, sub_rows):
    dot = functools.partial(jnp.dot, preferred_element_type=F32)
    tm = x_ref.shape[0]
    x = x_ref[...]
    ms = jnp.mean(x * x, axis=-1, keepdims=True)
    h = (x * lax.rsqrt(ms + NORM_EPS) * g_ref[...]).astype(BF16)
    h_sc[0] = h
    glr_ref[...] = dot(h, wglr_ref[...]).astype(BF16)
    for r0 in range(0, tm, sub_rows):
        rows = slice(r0, r0 + sub_rows)
        for order, dil in ((1, 4), (2, 16)):
            blk = h[rows, :].astype(F32).reshape(sub_rows // dil, dil, D_MODEL)
            h_sc[order, rows, :] = jnp.swapaxes(blk, 0, 1).reshape(sub_rows, D_MODEL).astype(BF16)

    w_refs = (w_ref, wgate_ref)
    plans = ((16, d3_ref, 2, t16_ref, _qkv_tiles(2, 0)),
             (4, d2_ref, 1, t4_ref, _qkv_tiles(1, 0)),
             (1, out_ref, 0, t1_ref, _qkv_tiles(0, CT_Q1)),
             (1, out_ref, 0, None, ((0, SRC_GQ, CT_GQ, False, True), (0, SRC_GK, CT_GK, False, False))),
             (1, out_ref, 0, None, _plain_tiles(0, SRC_GV, CT_GV) + _plain_tiles(0, SRC_GR, CT_GR)
              + _plain_tiles(1, 0, CT_GA) + _plain_tiles(1, 2, CT_GB)))
    for dilation, dst_ref, order, tab_ref, tiles in plans:
        per = sub_rows // dilation
        for weight, src_tile, dst_tile, rope, query in tiles:
            acc = dot(h_sc[order], w_refs[weight][:, src_tile * COL_TILE:(src_tile + 1) * COL_TILE])
            for hh in range(COL_TILE // HEAD_DIM):
                a = acc[:, hh * HEAD_DIM:(hh + 1) * HEAD_DIM]
                if rope:
                    up = pltpu.roll(a, HEAD_DIM - ROPE_DIM // 2, 1)
                    dn = pltpu.roll(a, ROPE_DIM // 2, 1)
                    a = a * tab_ref[0] + up * tab_ref[1] + dn * tab_ref[2]
                if query:
                    a = a * QUERY_SCALE
                a = a.astype(BF16)
                lo = dst_tile * COL_TILE + hh * HEAD_DIM
                if dilation == 1:
                    dst_ref[:, lo:lo + HEAD_DIM] = a
                else:
                    for s in range(tm // sub_rows):
                        for c in range(dilation):
                            src = s * sub_rows + c * per
                            dst_ref[0, c, s * per:(s + 1) * per, lo:lo + HEAD_DIM] = a[src:src + per, :]


def _rope_tables(seq, sub_rows, dilation):
    half = ROPE_DIM // 2
    row = lax.broadcasted_iota(jnp.int32, (seq, HEAD_DIM), 0)
    lane = lax.broadcasted_iota(jnp.int32, (seq, HEAD_DIM), 1)
    per = sub_rows // dilation
    within = row % sub_rows
    pos = (row - within) + dilation * (within % per) + within // per
    inv_freq = ROPE_THETA ** (-(2 * (lane % half)).astype(F32) / ROPE_DIM)
    ang = pos.astype(F32) * inv_freq
    cos, sin = jnp.cos(ang), jnp.sin(ang)
    return jnp.stack([jnp.where(lane < ROPE_DIM, cos, 1.0),
                      jnp.where(lane < half, -sin, 0.0),
                      jnp.where(jnp.logical_and(lane >= half, lane < ROPE_DIM), sin, 0.0)])


def _inproj(x2d, g1, w_main, w_gates, w_glr, batch, seq, tm, sub_rows):
    t = x2d.shape[0]
    pos_blocks = seq // tm
    resident = lambda shape: pl.BlockSpec(shape, lambda i: (0, 0), pipeline_mode=pl.Buffered(1))
    table_spec = pl.BlockSpec((3, tm, LANES), lambda i: (0, i % pos_blocks, 0))
    tab1, tab4, tab16 = (_rope_tables(seq, sub_rows, dilation) for dilation in (1, 4, 16))

    def dilated_spec(dilation):
        return pl.BlockSpec((1, dilation, tm // dilation, QKV_WIDTH),
                            lambda i: (i // pos_blocks, 0, i % pos_blocks, 0))

    return pl.pallas_call(
        functools.partial(_inproj_kernel, sub_rows=sub_rows),
        grid=(t // tm,),
        in_specs=[
            pl.BlockSpec((tm, D_MODEL), lambda i: (i, 0)),
            resident((1, D_MODEL)),
            resident((D_MODEL, SRC_GLR_COL)),
            resident(w_gates.shape),
            resident(w_glr.shape),
            table_spec, table_spec, table_spec,
        ],
        out_specs=[
            pl.BlockSpec((tm, MAIN_WIDTH), lambda i: (i, 0)),
            pl.BlockSpec((tm, LANES), lambda i: (i, 0)),
            dilated_spec(4),
            dilated_spec(16),
        ],
        out_shape=[
            jax.ShapeDtypeStruct((t, MAIN_WIDTH), BF16),
            jax.ShapeDtypeStruct((t, LANES), BF16),
            jax.ShapeDtypeStruct((batch, 4, seq // 4, QKV_WIDTH), BF16),
            jax.ShapeDtypeStruct((batch, 16, seq // 16, QKV_WIDTH), BF16),
        ],
        scratch_shapes=[pltpu.VMEM((3, tm, D_MODEL), BF16)],
        compiler_params=_params(("parallel",)),
        name="inproj",
    )(x2d, g1, w_main, w_gates, w_glr, tab1, tab4, tab16)


def _attn_kernel(q_ref, kp_ref, kc_ref, vp_ref, vc_ref, o_ref, st_ref, *scratch, tq, dilation):
    n = pl.program_id(1)
    blk = KEYS_BACK
    qi = lax.broadcasted_iota(jnp.int32, (blk, 2 * blk), 0)
    jj = lax.broadcasted_iota(jnp.int32, (blk, 2 * blk), 1)
    band = jnp.logical_and(jj >= qi, jj <= qi + blk)
    first_band = jnp.logical_and(band, jj + n * tq >= blk)
    lane = lax.broadcasted_iota(jnp.int32, (blk, LANES), 1)
    neg_inf = jnp.float32(-jnp.inf)

    def windows(c, qb, cur_ref, prev_ref, hs):
        if qb == 0:
            return jnp.concatenate([prev_ref[0, c, :, hs], cur_ref[0, c, 0:blk, hs]], axis=0)
        return cur_ref[0, c, (qb - 1) * blk:(qb + 1) * blk, hs]

    def scores(c, qb):
        valid = first_band if qb == 0 else band
        out = []
        for h in range(HEADS_PER_GROUP):
            hs = slice(h * HEAD_DIM, (h + 1) * HEAD_DIM)
            q = q_ref[0, c, qb * blk:(qb + 1) * blk, hs]
            kk = windows(c, qb, kc_ref, kp_ref, hs)
            s = lax.dot_general(q, kk, (((1,), (1,)), ((), ())), preferred_element_type=F32)
            out.append(jnp.where(valid, s, neg_inf))
        return out

    def finish(c, qb, s_heads):
        rows = slice(qb * blk, (qb + 1) * blk)
        nat = pl.ds(c + qb * blk * dilation, blk, stride=dilation)
        stats = jnp.zeros((blk, LANES), F32)
        for h, s in enumerate(s_heads):
            hs = slice(h * HEAD_DIM, (h + 1) * HEAD_DIM)
            vv = windows(c, qb, vc_ref, vp_ref, hs)
            m = jnp.max(s, axis=-1, keepdims=True)
            p = jnp.exp(s - m)
            l = jnp.sum(p, axis=-1, keepdims=True)
            o = jnp.dot(p.astype(BF16), vv, preferred_element_type=F32) / l
            if dilation == 1:
                o_ref[0, rows, hs] = o.astype(BF16)
            else:
                scratch[0][h, nat, :] = o
            stats = jnp.where(lane == h, m, stats)
            stats = jnp.where(lane == HEADS_PER_GROUP + h, l, stats)
        if dilation == 1:
            st_ref[0, rows, :] = stats
        else:
            st_ref[0, nat, :] = stats

    def run(blocks):
        pending = None
        for c, qb in blocks:
            s_heads = scores(c, qb)
            if pending is not None:
                finish(*pending)
            pending = (c, qb, s_heads)
        finish(*pending)

    if dilation == 1:
        run([(0, qb) for qb in range(tq // blk)])
    else:
        per_trip = max(1, BLOCK_HEADS_PER_TRIP // (HEADS_PER_GROUP * (tq // blk)))

        def body(t, carry):
            run([(t * per_trip + u, qb) for u in range(per_trip) for qb in range(tq // blk)])
            return carry
        lax.fori_loop(0, dilation // per_trip, body, 0)
        for h in range(HEADS_PER_GROUP):
            o_ref[0, :, h * HEAD_DIM:(h + 1) * HEAD_DIM] = scratch[0][h].astype(BF16)


def _attention_group(qkv, col_tiles, dilation, batch, seq, tile_positions):
    r = dilation
    tq = tile_positions // r
    nblk = seq // tile_positions
    prev_per_blk = tq // KEYS_BACK
    cq, ck, cv = col_tiles

    def cur(ct):
        return pl.BlockSpec((1, r, tq, GROUP_WIDTH), lambda b, n: (b, 0, n, ct))

    def prev(ct):
        return pl.BlockSpec((1, r, KEYS_BACK, GROUP_WIDTH),
                            lambda b, n: (b, 0, jnp.maximum(n * prev_per_blk - 1, 0), ct))

    scratch = []
    if r > 1:
        scratch.append(pltpu.VMEM((HEADS_PER_GROUP, tile_positions, HEAD_DIM), F32))
    o, st = pl.pallas_call(
        functools.partial(_attn_kernel, tq=tq, dilation=r),
        grid=(batch, nblk),
        in_specs=[cur(cq), prev(ck), cur(ck), prev(cv), cur(cv)],
        out_specs=[
            pl.BlockSpec((1, tile_positions, GROUP_WIDTH), lambda b, n: (b, n, 0)),
            pl.BlockSpec((1, tile_positions, LANES), lambda b, n: (b, n, 0)),
        ],
        out_shape=[
            jax.ShapeDtypeStruct((batch, seq, GROUP_WIDTH), BF16),
            jax.ShapeDtypeStruct((batch, seq, LANES), F32),
        ],
        scratch_shapes=scratch,
        compiler_params=_params(("parallel", "arbitrary")),
        name=f"attn_r{r}",
    )(qkv, qkv, qkv, qkv, qkv)
    return o.reshape(batch * seq, GROUP_WIDTH), st.reshape(batch * seq, LANES)


def _gla_kernel(q_ref, k_ref, v_ref, gr_ref, glr_ref, wg_ref, bg_ref, gn_ref, tri_ref,
                o_ref, state_sc, *, blk_rows, chunk):
    n = pl.program_id(1)

    @pl.when(n == 0)
    def _():
        state_sc[...] = jnp.zeros_like(state_sc)

    sub = GLA_SUB
    nsub = chunk // sub
    sub_shift = sub.bit_length() - 1
    row = lax.broadcasted_iota(jnp.int32, (chunk, chunk), 0)
    colm = lax.broadcasted_iota(jnp.int32, (chunk, chunk), 1)
    diag_mask = jnp.logical_and((row >> sub_shift) == (colm >> sub_shift), colm <= row)
    dn_t = (((1,), (1,)), ((), ()))
    dn_l = (((0,), (0,)), ((), ()))
    subs = [slice(sb * sub, (sb + 1) * sub) for sb in range(nsub)]
    rows_cat = functools.partial(jnp.concatenate, axis=0)

    def chunk_head(r0, bcum, c, h):
        rows = pl.ds(pl.multiple_of(r0 + c * chunk, chunk), chunk)
        ks = slice(h * GLA_KEY_DIM, (h + 1) * GLA_KEY_DIM)
        vs = slice(h * GLA_VAL_DIM, (h + 1) * GLA_VAL_DIM)
        b = bcum[c * chunk:(c + 1) * chunk, ks]
        q = q_ref[rows, ks].astype(F32)
        k = k_ref[rows, ks].astype(F32)
        v = v_ref[rows, vs]
        cen = [b[sb * sub + sub // 2:sb * sub + sub // 2 + 1, :] for sb in range(nsub)]
        bnd = [None] + [b[sb * sub - 1:sb * sub, :] for sb in range(1, nsub)]
        b_last = b[chunk - 1:chunk, :]
        dev = [b[s, :] - cen[sb] for sb, s in enumerate(subs)]
        q_c = [q[s, :] * jnp.exp(dev[sb]) for sb, s in enumerate(subs)]
        k_c = [k[s, :] * jnp.exp(-dev[sb]) for sb, s in enumerate(subs)]
        a = jnp.where(diag_mask,
                      lax.dot_general(rows_cat(q_c).astype(BF16), rows_cat(k_c).astype(BF16), dn_t,
                                      preferred_element_type=F32), 0.0)
        for sb in range(1, nsub):
            qb = (q_c[sb] * jnp.exp(cen[sb] - bnd[sb])).astype(BF16)
            kb = [k_c[t] * jnp.exp(bnd[sb] - cen[t]) for t in range(sb)]
            kb.append(jnp.zeros((chunk - sb * sub, GLA_KEY_DIM), F32))
            off = lax.dot_general(qb, rows_cat(kb).astype(BF16), dn_t, preferred_element_type=F32)
            pieces = [jnp.zeros((sb * sub, chunk), F32), off]
            if chunk - (sb + 1) * sub:
                pieces.append(jnp.zeros((chunk - (sb + 1) * sub, chunk), F32))
            a = a + rows_cat(pieces)
        q_in = rows_cat([q_c[sb] * jnp.exp(cen[sb]) for sb in range(nsub)]).astype(BF16)
        k_st = rows_cat([k_c[sb] * jnp.exp(b_last - cen[sb]) for sb in range(nsub)]).astype(BF16)
        st = state_sc[h]
        o_inter = lax.dot_general(q_in, st.astype(BF16), dn_t, preferred_element_type=F32)
        state_sc[h] = jnp.exp(b_last) * st + lax.dot_general(v, k_st, dn_l,
                                                               preferred_element_type=F32)
        o = o_inter + jnp.dot(a.astype(BF16), v, preferred_element_type=F32)
        ms = jnp.mean(o * o, axis=-1, keepdims=True)
        y = o * lax.rsqrt(ms + NORM_EPS) * gn_ref[...]
        g = gr_ref[rows, vs].astype(F32)
        o_ref[rows, vs] = (y * (g * jax.nn.sigmoid(g))).astype(BF16)

    def block(i, carry):
        r0 = pl.multiple_of(i * blk_rows, blk_rows)
        z = jnp.dot(glr_ref[pl.ds(r0, blk_rows), :], wg_ref[...],
                    preferred_element_type=F32) + bg_ref[...]
        log_a = (jnp.minimum(z, 0.0) - jnp.log(1.0 + jnp.exp(-jnp.abs(z)))) / GLA_GATE_NORMALIZER
        hi = log_a.astype(BF16)
        lo = (log_a - hi.astype(F32)).astype(BF16)
        bcum = (jnp.dot(tri_ref[...], hi, preferred_element_type=F32)
                + jnp.dot(tri_ref[...], lo, preferred_element_type=F32))
        for c in range(blk_rows // chunk):
            for h in range(GLA_HEADS):
                chunk_head(r0, bcum, c, h)
        return carry

    lax.fori_loop(0, q_ref.shape[0] // blk_rows, block, 0)


def _gla(proj, glr, wg, bg, gn, batch, seq, tc, blk_rows, chunk):
    tri = _block_tril(blk_rows, chunk)
    nblk = seq // tc
    row = lambda b, n: b * nblk + n
    return pl.pallas_call(
        functools.partial(_gla_kernel, blk_rows=blk_rows, chunk=chunk),
        grid=(batch, nblk),
        in_specs=[
            pl.BlockSpec((tc, COL_TILE), lambda b, n: (row(b, n), CT_GQ)),
            pl.BlockSpec((tc, COL_TILE), lambda b, n: (row(b, n), CT_GK)),
            pl.BlockSpec((tc, 2 * COL_TILE), lambda b, n: (row(b, n), CT_GV // 2)),
            pl.BlockSpec((tc, 2 * COL_TILE), lambda b, n: (row(b, n), CT_GR // 2)),
            pl.BlockSpec((tc, LANES), lambda b, n: (row(b, n), 0)),
            pl.BlockSpec((LANES, GLA_HEADS * GLA_KEY_DIM), lambda b, n: (0, 0)),
            pl.BlockSpec((1, GLA_HEADS * GLA_KEY_DIM), lambda b, n: (0, 0)),
            pl.BlockSpec((1, GLA_VAL_DIM), lambda b, n: (0, 0)),
            pl.BlockSpec((blk_rows, blk_rows), lambda b, n: (0, 0)),
        ],
        out_specs=pl.BlockSpec((tc, GLA_HEADS * GLA_VAL_DIM), lambda b, n: (row(b, n), 0)),
        out_shape=jax.ShapeDtypeStruct((batch * seq, GLA_HEADS * GLA_VAL_DIM), BF16),
        scratch_shapes=[pltpu.VMEM((GLA_HEADS, GLA_VAL_DIM, GLA_KEY_DIM), F32)],
        compiler_params=_params(("parallel", "arbitrary")),
        name="gla",
    )(proj, proj, proj, proj, glr, wg, bg, gn, tri)


def _rms(x, g):
    ms = jnp.mean(x * x, axis=-1, keepdims=True)
    return x * lax.rsqrt(ms + NORM_EPS) * g


def _post_kernel(x_ref, o1_ref, o2_ref, o3_ref, s1_ref, s2_ref, s3_ref, gla_ref, ga_ref, gb_ref,
                 wa_ref, wb_ref, wo_ref, g2_ref, wi_ref, wd_ref, gf_ref, out_ref, *, sub_rows,
                 ffn_bounds):
    o_refs = (o1_ref, o2_ref, o3_ref)
    s_refs = (s1_ref, s2_ref, s3_ref)
    subs = [slice(r0, r0 + sub_rows) for r0 in range(0, x_ref.shape[0], sub_rows)]
    chunks = list(zip(ffn_bounds[:-1], ffn_bounds[1:]))
    dot = functools.partial(jnp.dot, preferred_element_type=F32)

    def gla_branch(rows):
        return dot(gla_ref[rows, :], wb_ref[...])

    def attn_branch(rows):
        stats = [s[rows, :] for s in s_refs]
        heads = []
        for h in range(HEADS_PER_GROUP):
            hs = slice(h * HEAD_DIM, (h + 1) * HEAD_DIM)
            ms = [s[:, h:h + 1] for s in stats]
            ls = [s[:, HEADS_PER_GROUP + h:HEADS_PER_GROUP + h + 1] for s in stats]
            m_all = jnp.maximum(jnp.maximum(ms[0], ms[1]), ms[2])
            ws = [l * jnp.exp(m - m_all) for m, l in zip(ms, ls)]
            inv = 1.0 / (ws[0] + ws[1] + ws[2])
            acc = (ws[0] * inv) * o_refs[0][rows, hs].astype(F32)
            acc = acc + (ws[1] * inv) * o_refs[1][rows, hs].astype(F32)
            acc = acc + (ws[2] * inv) * o_refs[2][rows, hs].astype(F32)
            heads.append(acc.astype(BF16))
        return dot(jnp.concatenate(heads, axis=1), wa_ref[...])

    def mixer(rows, ya, yb):
        mix = (jax.nn.sigmoid(ga_ref[rows, :].astype(F32)) * ya
               + jax.nn.sigmoid(gb_ref[rows, :].astype(F32)) * yb)
        return x_ref[rows, :] + dot(mix.astype(BF16), wo_ref[...])

    def ffn_up(h2, lo, hi):
        g = dot(h2, wi_ref[:, lo:hi])
        u = dot(h2, wi_ref[:, FFN_HIDDEN + lo:FFN_HIDDEN + hi])
        return (g * jax.nn.sigmoid(g) * u).astype(BF16)

    yb = [gla_branch(rows) for rows in subs]
    ya = [attn_branch(rows) for rows in subs]
    x1 = [mixer(rows, a, b) for rows, a, b in zip(subs, ya, yb)]
    h2 = [_rms(v, g2_ref[...]).astype(BF16) for v in x1]
    acc = x1
    for lo, hi in chunks:
        act = [ffn_up(h, lo, hi) for h in h2]
        acc = [v + dot(a, wd_ref[lo:hi, :]) for v, a in zip(acc, act)]
    for rows, v in zip(subs, acc):
        out_ref[rows, :] = _rms(v, gf_ref[...])


def _post(x2d, attn_o, attn_st, gla_out, proj, wa, wb, wo, g2, wi, wd, gf, tm, sub_rows, ffn_bounds):
    t = x2d.shape[0]
    resident = lambda shape: pl.BlockSpec(shape, lambda i: (0, 0), pipeline_mode=pl.Buffered(1))
    rows = lambda width, col=0: pl.BlockSpec((tm, width), lambda i: (i, col))
    return pl.pallas_call(
        functools.partial(_post_kernel, sub_rows=sub_rows, ffn_bounds=ffn_bounds),
        grid=(t // tm,),
        in_specs=[
            rows(D_MODEL),
            rows(GROUP_WIDTH), rows(GROUP_WIDTH), rows(GROUP_WIDTH),
            rows(LANES), rows(LANES), rows(LANES),
            rows(D_MODEL),
            rows(D_MODEL, CT_GA // 2), rows(D_MODEL, CT_GB // 2),
            resident(wa.shape), resident(wb.shape), resident(wo.shape), resident(g2.shape),
            resident(wi.shape), resident(wd.shape), resident(gf.shape),
        ],
        out_specs=rows(D_MODEL),
        out_shape=jax.ShapeDtypeStruct((t, D_MODEL), F32),
        compiler_params=_params(("parallel",)),
        name="post",
    )(x2d, *attn_o, *attn_st, gla_out, proj, proj, wa, wb, wo, g2, wi, wd, gf)


def _block_tril(n, blk):
    r = np.arange(n)
    return jnp.asarray((r[:, None] // blk == r[None, :] // blk) & (r[None, :] <= r[:, None]), BF16)


def _layer(x2d, batch, seq, norm1_g, w_in, w_gate_lr, b_gate, gla_norm_g, w_branch_a, w_branch_b,
           w_out, norm2_g, w_ffn_in, w_ffn_down, out_g):
    assert w_in.shape[1] == SRC_GLR_COL + GLA_GATE_RANK + 2 * D_MODEL
    w_main = w_in.astype(BF16)
    w_glr = jnp.pad(w_main[:, SRC_GLR_COL:SRC_GLR_COL + GLA_GATE_RANK],
                    ((0, 0), (0, LANES - GLA_GATE_RANK)))
    w_gates = w_main[:, SRC_GLR_COL + GLA_GATE_RANK:]
    wg = jnp.pad(w_gate_lr, ((0, LANES - GLA_GATE_RANK), (0, 0))).astype(BF16)

    proj, glr, qkv2, qkv3 = _inproj(x2d, norm1_g[None, :], w_main, w_gates, w_glr,
                                    batch=batch, seq=seq, tm=512, sub_rows=256)

    qkv1 = proj.reshape(batch, 1, seq, MAIN_WIDTH)
    attn = [
        _attention_group(qkv1, (CT_Q1, CT_K1, CT_V1), 1, batch, seq, tile_positions=2048),
        _attention_group(qkv2, (0, 1, 2), 4, batch, seq, tile_positions=2048),
        _attention_group(qkv3, (0, 1, 2), 16, batch, seq, tile_positions=2048),
    ]
    attn_o = [o for o, _ in attn]
    attn_st = [st for _, st in attn]

    gla_out = _gla(proj, glr, wg, b_gate[None, :], gla_norm_g[None, :],
                   batch, seq, tc=2048, blk_rows=256, chunk=128)

    return _post(x2d, attn_o, attn_st, gla_out, proj,
                 w_branch_a.astype(BF16), w_branch_b.astype(BF16), w_out.astype(BF16),
                 norm2_g[None, :], w_ffn_in.astype(BF16), w_ffn_down.astype(BF16), out_g[None, :],
                 tm=512, sub_rows=256, ffn_bounds=(0, 1536, FFN_HIDDEN))


def kernel(x, norm1_g, w_in, w_gate_lr, b_gate, gla_norm_g, w_branch_a, w_branch_b, w_out, norm2_g,
           w_ffn_in, w_ffn_down, norm_f_g):
    batch, seq, d = x.shape
    depth = w_in.shape[0]
    assert depth == 1 and d == D_MODEL
    x2d = x.reshape(batch * seq, d)
    out = _layer(x2d, batch, seq, norm1_g[0], w_in[0], w_gate_lr[0], b_gate[0], gla_norm_g[0],
                 w_branch_a[0], w_branch_b[0], w_out[0], norm2_g[0], w_ffn_in[0], w_ffn_down[0],
                 norm_f_g)
    return out.reshape(batch, seq, d)
```

```python
import functools

import jax
import jax.numpy as jnp
import numpy as np
from jax import lax
from jax.experimental import pallas as pl
from jax.experimental.pallas import tpu as pltpu

F32 = jnp.float32
BF16 = jnp.bfloat16

D_MODEL = 1024
ATTN_GROUPS = ((128, 1), (512, 4), (2048, 16))
HEADS_PER_GROUP = 4
HEAD_DIM = 128
GROUP_WIDTH = HEADS_PER_GROUP * HEAD_DIM
KEYS_BACK = 128
BLOCK_HEADS_PER_TRIP = 64
ROPE_THETA = 500000.0
ROPE_DIM = HEAD_DIM // 4
GLA_HEADS = 4
GLA_KEY_DIM = 128
assert GLA_KEY_DIM == HEAD_DIM
QUERY_SCALE = HEAD_DIM ** -0.5
GLA_VAL_DIM = 256
GLA_GATE_RANK = 16
GLA_GATE_NORMALIZER = 16.0
GLA_SUB = 64
FFN_HIDDEN = 2816
NORM_EPS = 1e-6

LANES = 128
VMEM_LIMIT_BYTES = 56 * 1024 * 1024

COL_TILE = 512
CT_GV, CT_GR, CT_GA, CT_GB = 0, 2, 4, 6
CT_Q1, CT_K1, CT_V1 = 8, 9, 10
CT_GQ, CT_GK = 11, 12
N_MAIN_TILES = 13
MAIN_WIDTH = N_MAIN_TILES * COL_TILE
QKV_WIDTH = 3 * COL_TILE
SRC_AQ, SRC_AK, SRC_AV, SRC_GQ, SRC_GK, SRC_GV, SRC_GR = 0, 3, 6, 9, 10, 11, 13
SRC_GLR_COL = 15 * COL_TILE


def _params(semantics):
    return pltpu.CompilerParams(dimension_semantics=semantics, vmem_limit_bytes=VMEM_LIMIT_BYTES)


def _qkv_tiles(group, first_dst):
    return ((0, SRC_AQ + group, first_dst, True, True), (0, SRC_AK + group, first_dst + 1, True, False),
            (0, SRC_AV + group, first_dst + 2, False, False))


def _plain_tiles(weight, first_src, first_dst, count=2):
    return tuple((weight, first_src + t, first_dst + t, False, False) for t in range(count))


def _inproj_kernel(x_ref, g_ref, w_ref, wgate_ref, wglr_ref, t1_ref, t4_ref, t16_ref,
                   out_ref, glr_ref, d2_ref, d3_ref, h_sc, *, sub_rows):
    dot = functools.partial(jnp.dot, preferred_element_type=F32)
    tm = x_ref.shape[0]
    x = x_ref[...]
    ms = jnp.mean(x * x, axis=-1, keepdims=True)
    h = (x * lax.rsqrt(ms + NORM_EPS) * g_ref[...]).astype(BF16)
    h_sc[0] = h
    glr_ref[...] = dot(h, wglr_ref[...]).astype(BF16)
    for r0 in range(0, tm, sub_rows):
        rows = slice(r0, r0 + sub_rows)
        for order, dil in ((1, 4), (2, 16)):
            blk = h[rows, :].astype(F32).reshape(sub_rows // dil, dil, D_MODEL)
            h_sc[order, rows, :] = jnp.swapaxes(blk, 0, 1).reshape(sub_rows, D_MODEL).astype(BF16)

    w_refs = (w_ref, wgate_ref)
    plans = ((1, out_ref, 0, t1_ref, _qkv_tiles(0, CT_Q1)),
             (1, out_ref, 0, None, ((0, SRC_GQ, CT_GQ, False, True), (0, SRC_GK, CT_GK, False, False))),
             (1, out_ref, 0, None, _plain_tiles(0, SRC_GV, CT_GV) + _plain_tiles(0, SRC_GR, CT_GR)),
             (4, d2_ref, 1, t4_ref, _qkv_tiles(1, 0)),
             (16, d3_ref, 2, t16_ref, _qkv_tiles(2, 0)),
             (1, out_ref, 0, None, _plain_tiles(1, 0, CT_GA) + _plain_tiles(1, 2, CT_GB)))
    for dilation, dst_ref, order, tab_ref, tiles in plans:
        per = sub_rows // dilation
        for weight, src_tile, dst_tile, rope, query in tiles:
            acc = dot(h_sc[order], w_refs[weight][:, src_tile * COL_TILE:(src_tile + 1) * COL_TILE])
            for hh in range(COL_TILE // HEAD_DIM):
                a = acc[:, hh * HEAD_DIM:(hh + 1) * HEAD_DIM]
                if rope:
                    up = pltpu.roll(a, HEAD_DIM - ROPE_DIM // 2, 1)
                    dn = pltpu.roll(a, ROPE_DIM // 2, 1)
                    a = a * tab_ref[0] + up * tab_ref[1] + dn * tab_ref[2]
                if query:
                    a = a * QUERY_SCALE
                a = a.astype(BF16)
                lo = dst_tile * COL_TILE + hh * HEAD_DIM
                if dilation == 1:
                    dst_ref[:, lo:lo + HEAD_DIM] = a
                else:
                    for s in range(tm // sub_rows):
                        for c in range(dilation):
                            src = s * sub_rows + c * per
                            dst_ref[0, c, s * per:(s + 1) * per, lo:lo + HEAD_DIM] = a[src:src + per, :]


def _rope_tables(seq, sub_rows, dilation):
    half = ROPE_DIM // 2
    row = lax.broadcasted_iota(jnp.int32, (seq, HEAD_DIM), 0)
    lane = lax.broadcasted_iota(jnp.int32, (seq, HEAD_DIM), 1)
    per = sub_rows // dilation
    within = row % sub_rows
    pos = (row - within) + dilation * (within % per) + within // per
    inv_freq = ROPE_THETA ** (-(2 * (lane % half)).astype(F32) / ROPE_DIM)
    ang = pos.astype(F32) * inv_freq
    cos, sin = jnp.cos(ang), jnp.sin(ang)
    return jnp.stack([jnp.where(lane < ROPE_DIM, cos, 1.0),
                      jnp.where(lane < half, -sin, 0.0),
                      jnp.where(jnp.logical_and(lane >= half, lane < ROPE_DIM), sin, 0.0)])


def _inproj(x2d, g1, w_main, w_gates, w_glr, batch, seq, tm, sub_rows):
    t = x2d.shape[0]
    pos_blocks = seq // tm
    resident = lambda shape: pl.BlockSpec(shape, lambda i: (0, 0), pipeline_mode=pl.Buffered(1))
    table_spec = pl.BlockSpec((3, tm, LANES), lambda i: (0, i % pos_blocks, 0))
    tab1, tab4, tab16 = (_rope_tables(seq, sub_rows, dilation) for dilation in (1, 4, 16))

    def dilated_spec(dilation):
        return pl.BlockSpec((1, dilation, tm // dilation, QKV_WIDTH),
                            lambda i: (i // pos_blocks, 0, i % pos_blocks, 0))

    return pl.pallas_call(
        functools.partial(_inproj_kernel, sub_rows=sub_rows),
        grid=(t // tm,),
        in_specs=[
            pl.BlockSpec((tm, D_MODEL), lambda i: (i, 0)),
            resident((1, D_MODEL)),
            resident((D_MODEL, SRC_GLR_COL)),
            resident(w_gates.shape),
            resident(w_glr.shape),
            table_spec, table_spec, table_spec,
        ],
        out_specs=[
            pl.BlockSpec((tm, MAIN_WIDTH), lambda i: (i, 0)),
            pl.BlockSpec((tm, LANES), lambda i: (i, 0)),
            dilated_spec(4),
            dilated_spec(16),
        ],
        out_shape=[
            jax.ShapeDtypeStruct((t, MAIN_WIDTH), BF16),
            jax.ShapeDtypeStruct((t, LANES), BF16),
            jax.ShapeDtypeStruct((batch, 4, seq // 4, QKV_WIDTH), BF16),
            jax.ShapeDtypeStruct((batch, 16, seq // 16, QKV_WIDTH), BF16),
        ],
        scratch_shapes=[pltpu.VMEM((3, tm, D_MODEL), BF16)],
        compiler_params=_params(("parallel",)),
        name="inproj",
    )(x2d, g1, w_main, w_gates, w_glr, tab1, tab4, tab16)


def _attn_kernel(q_ref, kp_ref, kc_ref, vp_ref, vc_ref, o_ref, st_ref, *scratch, tq, dilation):
    n = pl.program_id(1)
    blk = KEYS_BACK
    qi = lax.broadcasted_iota(jnp.int32, (blk, 2 * blk), 0)
    jj = lax.broadcasted_iota(jnp.int32, (blk, 2 * blk), 1)
    band = jnp.logical_and(jj >= qi, jj <= qi + blk)
    first_band = jnp.logical_and(band, jj + n * tq >= blk)
    lane = lax.broadcasted_iota(jnp.int32, (blk, LANES), 1)
    neg_inf = jnp.float32(-jnp.inf)

    def windows(c, qb, cur_ref, prev_ref, hs):
        if qb == 0:
            return jnp.concatenate([prev_ref[0, c, :, hs], cur_ref[0, c, 0:blk, hs]], axis=0)
        return cur_ref[0, c, (qb - 1) * blk:(qb + 1) * blk, hs]

    def scores(c, qb):
        valid = first_band if qb == 0 else band
        out = []
        for h in range(HEADS_PER_GROUP):
            hs = slice(h * HEAD_DIM, (h + 1) * HEAD_DIM)
            q = q_ref[0, c, qb * blk:(qb + 1) * blk, hs]
            kk = windows(c, qb, kc_ref, kp_ref, hs)
            s = lax.dot_general(q, kk, (((1,), (1,)), ((), ())), preferred_element_type=F32)
            out.append(jnp.where(valid, s, neg_inf))
        return out

    def finish(c, qb, s_heads):
        rows = slice(qb * blk, (qb + 1) * blk)
        nat = pl.ds(c + qb * blk * dilation, blk, stride=dilation)
        stats = jnp.zeros((blk, LANES), F32)
        for h, s in enumerate(s_heads):
            hs = slice(h * HEAD_DIM, (h + 1) * HEAD_DIM)
            vv = windows(c, qb, vc_ref, vp_ref, hs)
            m = jnp.max(s, axis=-1, keepdims=True)
            p = jnp.exp(s - m)
            l = jnp.sum(p, axis=-1, keepdims=True)
            o = jnp.dot(p.astype(BF16), vv, preferred_element_type=F32) / l
            if dilation == 1:
                o_ref[0, rows, hs] = o.astype(BF16)
            else:
                scratch[0][h, nat, :] = o
            stats = jnp.where(lane == h, m, stats)
            stats = jnp.where(lane == HEADS_PER_GROUP + h, l, stats)
        if dilation == 1:
            st_ref[0, rows, :] = stats
        else:
            st_ref[0, nat, :] = stats

    def run(blocks):
        pending = None
        for c, qb in blocks:
            s_heads = scores(c, qb)
            if pending is not None:
                finish(*pending)
            pending = (c, qb, s_heads)
        finish(*pending)

    if dilation == 1:
        run([(0, qb) for qb in range(tq // blk)])
    else:
        per_trip = max(1, BLOCK_HEADS_PER_TRIP // (HEADS_PER_GROUP * (tq // blk)))

        def body(t, carry):
            run([(t * per_trip + u, qb) for u in range(per_trip) for qb in range(tq // blk)])
            return carry
        lax.fori_loop(0, dilation // per_trip, body, 0)
        for h in range(HEADS_PER_GROUP):
            o_ref[0, :, h * HEAD_DIM:(h + 1) * HEAD_DIM] = scratch[0][h].astype(BF16)


def _attention_group(qkv, col_tiles, dilation, batch, seq, tile_positions):
    r = dilation
    tq = tile_positions // r
    nblk = seq // tile_positions
    prev_per_blk = tq // KEYS_BACK
    cq, ck, cv = col_tiles

    def cur(ct):
        return pl.BlockSpec((1, r, tq, GROUP_WIDTH), lambda b, n: (b, 0, n, ct))

    def prev(ct):
        return pl.BlockSpec((1, r, KEYS_BACK, GROUP_WIDTH),
                            lambda b, n: (b, 0, jnp.maximum(n * prev_per_blk - 1, 0), ct))

    scratch = []
    if r > 1:
        scratch.append(pltpu.VMEM((HEADS_PER_GROUP, tile_positions, HEAD_DIM), F32))
    o, st = pl.pallas_call(
        functools.partial(_attn_kernel, tq=tq, dilation=r),
        grid=(batch, nblk),
        in_specs=[cur(cq), prev(ck), cur(ck), prev(cv), cur(cv)],
        out_specs=[
            pl.BlockSpec((1, tile_positions, GROUP_WIDTH), lambda b, n: (b, n, 0)),
            pl.BlockSpec((1, tile_positions, LANES), lambda b, n: (b, n, 0)),
        ],
        out_shape=[
            jax.ShapeDtypeStruct((batch, seq, GROUP_WIDTH), BF16),
            jax.ShapeDtypeStruct((batch, seq, LANES), F32),
        ],
        scratch_shapes=scratch,
        compiler_params=_params(("parallel", "arbitrary")),
        name=f"attn_r{r}",
    )(qkv, qkv, qkv, qkv, qkv)
    return o.reshape(batch * seq, GROUP_WIDTH), st.reshape(batch * seq, LANES)


def _gla_kernel(q_ref, k_ref, v_ref, gr_ref, glr_ref, wg_ref, bg_ref, gn_ref, tri_ref,
                o_ref, state_sc, *, blk_rows, chunk):
    n = pl.program_id(1)

    @pl.when(n == 0)
    def _():
        state_sc[...] = jnp.zeros_like(state_sc)

    sub = GLA_SUB
    nsub = chunk // sub
    sub_shift = sub.bit_length() - 1
    row = lax.broadcasted_iota(jnp.int32, (chunk, chunk), 0)
    colm = lax.broadcasted_iota(jnp.int32, (chunk, chunk), 1)
    diag_mask = jnp.logical_and((row >> sub_shift) == (colm >> sub_shift), colm <= row)
    dn_t = (((1,), (1,)), ((), ()))
    dn_l = (((0,), (0,)), ((), ()))
    subs = [slice(sb * sub, (sb + 1) * sub) for sb in range(nsub)]
    rows_cat = functools.partial(jnp.concatenate, axis=0)

    def chunk_head(r0, bcum, c, h):
        rows = pl.ds(pl.multiple_of(r0 + c * chunk, chunk), chunk)
        ks = slice(h * GLA_KEY_DIM, (h + 1) * GLA_KEY_DIM)
        vs = slice(h * GLA_VAL_DIM, (h + 1) * GLA_VAL_DIM)
        b = bcum[c * chunk:(c + 1) * chunk, ks]
        q = q_ref[rows, ks].astype(F32)
        k = k_ref[rows, ks].astype(F32)
        v = v_ref[rows, vs]
        cen = [b[sb * sub + sub // 2:sb * sub + sub // 2 + 1, :] for sb in range(nsub)]
        bnd = [None] + [b[sb * sub - 1:sb * sub, :] for sb in range(1, nsub)]
        b_last = b[chunk - 1:chunk, :]
        dev = [b[s, :] - cen[sb] for sb, s in enumerate(subs)]
        q_c = [q[s, :] * jnp.exp(dev[sb]) for sb, s in enumerate(subs)]
        k_c = [k[s, :] * jnp.exp(-dev[sb]) for sb, s in enumerate(subs)]
        a = jnp.where(diag_mask,
                      lax.dot_general(rows_cat(q_c).astype(BF16), rows_cat(k_c).astype(BF16), dn_t,
                                      preferred_element_type=F32), 0.0)
        for sb in range(1, nsub):
            qb = (q_c[sb] * jnp.exp(cen[sb] - bnd[sb])).astype(BF16)
            kb = [k_c[t] * jnp.exp(bnd[sb] - cen[t]) for t in range(sb)]
            kb.append(jnp.zeros((chunk - sb * sub, GLA_KEY_DIM), F32))
            off = lax.dot_general(qb, rows_cat(kb).astype(BF16), dn_t, preferred_element_type=F32)
            pieces = [jnp.zeros((sb * sub, chunk), F32), off]
            if chunk - (sb + 1) * sub:
                pieces.append(jnp.zeros((chunk - (sb + 1) * sub, chunk), F32))
            a = a + rows_cat(pieces)
        q_in = rows_cat([q_c[sb] * jnp.exp(cen[sb]) for sb in range(nsub)]).astype(BF16)
        k_st = rows_cat([k_c[sb] * jnp.exp(b_last - cen[sb]) for sb in range(nsub)]).astype(BF16)
        st = state_sc[h]
        o_inter = lax.dot_general(q_in, st.astype(BF16), dn_t, preferred_element_type=F32)
        state_sc[h] = jnp.exp(b_last) * st + lax.dot_general(v, k_st, dn_l,
                                                               preferred_element_type=F32)
        o = o_inter + jnp.dot(a.astype(BF16), v, preferred_element_type=F32)
        ms = jnp.mean(o * o, axis=-1, keepdims=True)
        y = o * lax.rsqrt(ms + NORM_EPS) * gn_ref[...]
        g = gr_ref[rows, vs].astype(F32)
        o_ref[rows, vs] = (y * (g * jax.nn.sigmoid(g))).astype(BF16)

    def block(i, carry):
        r0 = pl.multiple_of(i * blk_rows, blk_rows)
        z = jnp.dot(glr_ref[pl.ds(r0, blk_rows), :], wg_ref[...],
                    preferred_element_type=F32) + bg_ref[...]
        log_a = (jnp.minimum(z, 0.0) - jnp.log(1.0 + jnp.exp(-jnp.abs(z)))) / GLA_GATE_NORMALIZER
        hi = log_a.astype(BF16)
        lo = (log_a - hi.astype(F32)).astype(BF16)
        bcum = (jnp.dot(tri_ref[...], hi, preferred_element_type=F32)
                + jnp.dot(tri_ref[...], lo, preferred_element_type=F32))
        for c in range(blk_rows // chunk):
            for h in range(GLA_HEADS):
                chunk_head(r0, bcum, c, h)
        return carry

    lax.fori_loop(0, q_ref.shape[0] // blk_rows, block, 0, unroll=2)


def _gla(proj, glr, wg, bg, gn, batch, seq, tc, blk_rows, chunk):
    tri = _block_tril(blk_rows, chunk)
    nblk = seq // tc
    row = lambda b, n: b * nblk + n
    return pl.pallas_call(
        functools.partial(_gla_kernel, blk_rows=blk_rows, chunk=chunk),
        grid=(batch, nblk),
        in_specs=[
            pl.BlockSpec((tc, COL_TILE), lambda b, n: (row(b, n), CT_GQ)),
            pl.BlockSpec((tc, COL_TILE), lambda b, n: (row(b, n), CT_GK)),
            pl.BlockSpec((tc, 2 * COL_TILE), lambda b, n: (row(b, n), CT_GV // 2)),
            pl.BlockSpec((tc, 2 * COL_TILE), lambda b, n: (row(b, n), CT_GR // 2)),
            pl.BlockSpec((tc, LANES), lambda b, n: (row(b, n), 0)),
            pl.BlockSpec((LANES, GLA_HEADS * GLA_KEY_DIM), lambda b, n: (0, 0)),
            pl.BlockSpec((1, GLA_HEADS * GLA_KEY_DIM), lambda b, n: (0, 0)),
            pl.BlockSpec((1, GLA_VAL_DIM), lambda b, n: (0, 0)),
            pl.BlockSpec((blk_rows, blk_rows), lambda b, n: (0, 0)),
        ],
        out_specs=pl.BlockSpec((tc, GLA_HEADS * GLA_VAL_DIM), lambda b, n: (row(b, n), 0)),
        out_shape=jax.ShapeDtypeStruct((batch * seq, GLA_HEADS * GLA_VAL_DIM), BF16),
        scratch_shapes=[pltpu.VMEM((GLA_HEADS, GLA_VAL_DIM, GLA_KEY_DIM), F32)],
        compiler_params=_params(("parallel", "arbitrary")),
        name="gla",
    )(proj, proj, proj, proj, glr, wg, bg, gn, tri)


def _rms(x, g):
    ms = jnp.mean(x * x, axis=-1, keepdims=True)
    return x * lax.rsqrt(ms + NORM_EPS) * g


def _post_kernel(x_ref, o1_ref, o2_ref, o3_ref, s1_ref, s2_ref, s3_ref, gla_ref, ga_ref, gb_ref,
                 wa_ref, wb_ref, wo_ref, g2_ref, wi_ref, wd_ref, gf_ref, out_ref, *, sub_rows,
                 ffn_bounds):
    o_refs = (o1_ref, o2_ref, o3_ref)
    s_refs = (s1_ref, s2_ref, s3_ref)
    subs = [slice(r0, r0 + sub_rows) for r0 in range(0, x_ref.shape[0], sub_rows)]
    chunks = list(zip(ffn_bounds[:-1], ffn_bounds[1:]))
    dot = functools.partial(jnp.dot, preferred_element_type=F32)

    def gla_branch(rows):
        return dot(gla_ref[rows, :], wb_ref[...])

    def attn_branch(rows):
        stats = [s[rows, :] for s in s_refs]
        heads = []
        for h in range(HEADS_PER_GROUP):
            hs = slice(h * HEAD_DIM, (h + 1) * HEAD_DIM)
            ms = [s[:, h:h + 1] for s in stats]
            ls = [s[:, HEADS_PER_GROUP + h:HEADS_PER_GROUP + h + 1] for s in stats]
            m_all = jnp.maximum(jnp.maximum(ms[0], ms[1]), ms[2])
            ws = [l * jnp.exp(m - m_all) for m, l in zip(ms, ls)]
            inv = 1.0 / (ws[0] + ws[1] + ws[2])
            acc = (ws[0] * inv) * o_refs[0][rows, hs].astype(F32)
            acc = acc + (ws[1] * inv) * o_refs[1][rows, hs].astype(F32)
            acc = acc + (ws[2] * inv) * o_refs[2][rows, hs].astype(F32)
            heads.append(acc.astype(BF16))
        return dot(jnp.concatenate(heads, axis=1), wa_ref[...])

    def mixer(rows, ya, yb):
        mix = (jax.nn.sigmoid(ga_ref[rows, :].astype(F32)) * ya
               + jax.nn.sigmoid(gb_ref[rows, :].astype(F32)) * yb)
        return x_ref[rows, :] + dot(mix.astype(BF16), wo_ref[...])

    def ffn_up(h2, lo, hi):
        g = dot(h2, wi_ref[:, lo:hi])
        u = dot(h2, wi_ref[:, FFN_HIDDEN + lo:FFN_HIDDEN + hi])
        return (g * jax.nn.sigmoid(g) * u).astype(BF16)

    yb = [gla_branch(rows) for rows in subs]
    ya = [attn_branch(rows) for rows in subs]
    x1 = [mixer(rows, a, b) for rows, a, b in zip(subs, ya, yb)]
    h2 = [_rms(v, g2_ref[...]).astype(BF16) for v in x1]
    acc = x1
    for lo, hi in chunks:
        act = [ffn_up(h, lo, hi) for h in h2]
        acc = [v + dot(a, wd_ref[lo:hi, :]) for v, a in zip(acc, act)]
    for rows, v in zip(subs, acc):
        out_ref[rows, :] = _rms(v, gf_ref[...])


def _post(x2d, attn_o, attn_st, gla_out, proj, wa, wb, wo, g2, wi, wd, gf, tm, sub_rows, ffn_bounds):
    t = x2d.shape[0]
    resident = lambda shape: pl.BlockSpec(shape, lambda i: (0, 0), pipeline_mode=pl.Buffered(1))
    rows = lambda width, col=0: pl.BlockSpec((tm, width), lambda i: (i, col))
    return pl.pallas_call(
        functools.partial(_post_kernel, sub_rows=sub_rows, ffn_bounds=ffn_bounds),
        grid=(t // tm,),
        in_specs=[
            rows(D_MODEL),
            rows(GROUP_WIDTH), rows(GROUP_WIDTH), rows(GROUP_WIDTH),
            rows(LANES), rows(LANES), rows(LANES),
            rows(D_MODEL),
            rows(D_MODEL, CT_GA // 2), rows(D_MODEL, CT_GB // 2),
            resident(wa.shape), resident(wb.shape), resident(wo.shape), resident(g2.shape),
            resident(wi.shape), resident(wd.shape), resident(gf.shape),
        ],
        out_specs=rows(D_MODEL),
        out_shape=jax.ShapeDtypeStruct((t, D_MODEL), F32),
        compiler_params=_params(("parallel",)),
        name="post",
    )(x2d, *attn_o, *attn_st, gla_out, proj, proj, wa, wb, wo, g2, wi, wd, gf)


def _block_tril(n, blk):
    r = np.arange(n)
    return jnp.asarray((r[:, None] // blk == r[None, :] // blk) & (r[None, :] <= r[:, None]), BF16)


def _layer(x2d, batch, seq, norm1_g, w_in, w_gate_lr, b_gate, gla_norm_g, w_branch_a, w_branch_b,
           w_out, norm2_g, w_ffn_in, w_ffn_down, out_g):
    assert w_in.shape[1] == SRC_GLR_COL + GLA_GATE_RANK + 2 * D_MODEL
    w_main = w_in.astype(BF16)
    w_glr = jnp.pad(w_main[:, SRC_GLR_COL:SRC_GLR_COL + GLA_GATE_RANK],
                    ((0, 0), (0, LANES - GLA_GATE_RANK)))
    w_gates = w_main[:, SRC_GLR_COL + GLA_GATE_RANK:]
    wg = jnp.pad(w_gate_lr, ((0, LANES - GLA_GATE_RANK), (0, 0))).astype(BF16)

    proj, glr, qkv2, qkv3 = _inproj(x2d, norm1_g[None, :], w_main, w_gates, w_glr,
                                    batch=batch, seq=seq, tm=512, sub_rows=256)

    qkv1 = proj.reshape(batch, 1, seq, MAIN_WIDTH)
    attn = [
        _attention_group(qkv1, (CT_Q1, CT_K1, CT_V1), 1, batch, seq, tile_positions=2048),
        _attention_group(qkv2, (0, 1, 2), 4, batch, seq, tile_positions=2048),
        _attention_group(qkv3, (0, 1, 2), 16, batch, seq, tile_positions=2048),
    ]
    attn_o = [o for o, _ in attn]
    attn_st = [st for _, st in attn]

    gla_out = _gla(proj, glr, wg, b_gate[None, :], gla_norm_g[None, :],
                   batch, seq, tc=2048, blk_rows=256, chunk=128)

    return _post(x2d, attn_o, attn_st, gla_out, proj,
                 w_branch_a.astype(BF16), w_branch_b.astype(BF16), w_out.astype(BF16),
                 norm2_g[None, :], w_ffn_in.astype(BF16), w_ffn_down.astype(BF16), out_g[None, :],
                 tm=512, sub_rows=256, ffn_bounds=(0, 1536, FFN_HIDDEN))


def kernel(x, norm1_g, w_in, w_gate_lr, b_gate, gla_norm_g, w_branch_a, w_branch_b, w_out, norm2_g,
           w_ffn_in, w_ffn_down, norm_f_g):
    batch, seq, d = x.shape
    depth = w_in.shape[0]
    assert depth == 1 and d == D_MODEL
    x2d = x.reshape(batch * seq, d)
    out = _layer(x2d, batch, seq, norm1_g[0], w_in[0], w_gate_lr[0], b_gate[0], gla_norm_g[0],
                 w_branch_a[0], w_branch_b[0], w_out[0], norm2_g[0], w_ffn_in[0], w_ffn_down[0],
                 norm_f_g)
    return out.reshape(batch, seq, d)
```

```python
import functools

import jax
import jax.numpy as jnp
import numpy as np
from jax import lax
from jax.experimental import pallas as pl
from jax.experimental.pallas import tpu as pltpu

F32 = jnp.float32
BF16 = jnp.bfloat16

D_MODEL = 1024
ATTN_GROUPS = ((128, 1), (512, 4), (2048, 16))
HEADS_PER_GROUP = 4
HEAD_DIM = 128
GROUP_WIDTH = HEADS_PER_GROUP * HEAD_DIM
KEYS_BACK = 128
BLOCK_HEADS_PER_TRIP = 64
ROPE_THETA = 500000.0
ROPE_DIM = HEAD_DIM // 4
GLA_HEADS = 4
GLA_KEY_DIM = 128
assert GLA_KEY_DIM == HEAD_DIM
QUERY_SCALE = HEAD_DIM ** -0.5
GLA_VAL_DIM = 256
GLA_GATE_RANK = 16
GLA_GATE_NORMALIZER = 16.0
GLA_SUB = 64
FFN_HIDDEN = 2816
NORM_EPS = 1e-6

LANES = 128
VMEM_LIMIT_BYTES = 56 * 1024 * 1024

ROW_TILE = 512
SUB_ROWS = 256
ATTN_TILE_POSITIONS = 2048
GLA_STEP_ROWS = 2048
GLA_BLOCK_ROWS = 256
GLA_CHUNK = 128
GLA_BLOCKS_PER_TRIP = 4

COL_TILE = 512
CT_GV, CT_GR, CT_GA, CT_GB = 0, 2, 4, 6
CT_Q1, CT_K1, CT_V1 = 8, 9, 10
CT_GQ, CT_GK = 11, 12
N_MAIN_TILES = 13
MAIN_WIDTH = N_MAIN_TILES * COL_TILE
QKV_WIDTH = 3 * COL_TILE
SRC_AQ, SRC_AK, SRC_AV, SRC_GQ, SRC_GK, SRC_GV, SRC_GR = 0, 3, 6, 9, 10, 11, 13
SRC_GLR_COL = 15 * COL_TILE


def _params(semantics):
    return pltpu.CompilerParams(dimension_semantics=semantics, vmem_limit_bytes=VMEM_LIMIT_BYTES)


def _qkv_tiles(group, first_dst):
    return ((0, SRC_AQ + group, first_dst, True, True), (0, SRC_AK + group, first_dst + 1, True, False),
            (0, SRC_AV + group, first_dst + 2, False, False))


def _plain_tiles(weight, first_src, first_dst, count=2):
    return tuple((weight, first_src + t, first_dst + t, False, False) for t in range(count))


def _inproj_kernel(x_ref, g_ref, w_ref, wgate_ref, wglr_ref, t1_ref, t4_ref, t16_ref,
                   out_ref, glr_ref, d2_ref, d3_ref, h_sc, *, sub_rows):
    dot = functools.partial(jnp.dot, preferred_element_type=F32)
    tm = x_ref.shape[0]
    x = x_ref[...]
    ms = jnp.mean(x * x, axis=-1, keepdims=True)
    h = (x * lax.rsqrt(ms + NORM_EPS) * g_ref[...]).astype(BF16)
    h_sc[0] = h
    glr_ref[...] = dot(h, wglr_ref[...]).astype(BF16)
    for r0 in range(0, tm, sub_rows):
        rows = slice(r0, r0 + sub_rows)
        for order, dil in ((1, 4), (2, 16)):
            blk = h[rows, :].astype(F32).reshape(sub_rows // dil, dil, D_MODEL)
            h_sc[order, rows, :] = jnp.swapaxes(blk, 0, 1).reshape(sub_rows, D_MODEL).astype(BF16)

    w_refs = (w_ref, wgate_ref)
    plans = ((1, out_ref, 0, t1_ref, _qkv_tiles(0, CT_Q1)),
             (1, out_ref, 0, None, ((0, SRC_GQ, CT_GQ, False, True), (0, SRC_GK, CT_GK, False, False))),
             (1, out_ref, 0, None, _plain_tiles(0, SRC_GV, CT_GV) + _plain_tiles(0, SRC_GR, CT_GR)),
             (4, d2_ref, 1, t4_ref, _qkv_tiles(1, 0)),
             (16, d3_ref, 2, t16_ref, _qkv_tiles(2, 0)),
             (1, out_ref, 0, None, _plain_tiles(1, 0, CT_GA) + _plain_tiles(1, 2, CT_GB)))
    for dilation, dst_ref, order, tab_ref, tiles in plans:
        per = sub_rows // dilation
        for weight, src_tile, dst_tile, rope, query in tiles:
            acc = dot(h_sc[order], w_refs[weight][:, src_tile * COL_TILE:(src_tile + 1) * COL_TILE])
            for hh in range(COL_TILE // HEAD_DIM):
                a = acc[:, hh * HEAD_DIM:(hh + 1) * HEAD_DIM]
                if rope:
                    up = pltpu.roll(a, HEAD_DIM - ROPE_DIM // 2, 1)
                    dn = pltpu.roll(a, ROPE_DIM // 2, 1)
                    a = a * tab_ref[0] + up * tab_ref[1] + dn * tab_ref[2]
                if query:
                    a = a * QUERY_SCALE
                a = a.astype(BF16)
                lo = dst_tile * COL_TILE + hh * HEAD_DIM
                if dilation == 1:
                    dst_ref[:, lo:lo + HEAD_DIM] = a
                else:
                    for s in range(tm // sub_rows):
                        for c in range(dilation):
                            src = s * sub_rows + c * per
                            dst_ref[0, c, s * per:(s + 1) * per, lo:lo + HEAD_DIM] = a[src:src + per, :]


def _rope_tables(seq, sub_rows, dilation):
    half = ROPE_DIM // 2
    row = lax.broadcasted_iota(jnp.int32, (seq, HEAD_DIM), 0)
    lane = lax.broadcasted_iota(jnp.int32, (seq, HEAD_DIM), 1)
    per = sub_rows // dilation
    within = row % sub_rows
    pos = (row - within) + dilation * (within % per) + within // per
    inv_freq = ROPE_THETA ** (-(2 * (lane % half)).astype(F32) / ROPE_DIM)
    ang = pos.astype(F32) * inv_freq
    cos, sin = jnp.cos(ang), jnp.sin(ang)
    return jnp.stack([jnp.where(lane < ROPE_DIM, cos, 1.0),
                      jnp.where(lane < half, -sin, 0.0),
                      jnp.where(jnp.logical_and(lane >= half, lane < ROPE_DIM), sin, 0.0)])


def _inproj(x2d, g1, w_main, w_gates, w_glr, batch, seq, tm, sub_rows):
    t = x2d.shape[0]
    pos_blocks = seq // tm
    resident = lambda shape: pl.BlockSpec(shape, lambda i: (0, 0), pipeline_mode=pl.Buffered(1))
    table_spec = pl.BlockSpec((3, tm, LANES), lambda i: (0, i % pos_blocks, 0))
    tab1, tab4, tab16 = (_rope_tables(seq, sub_rows, dilation) for dilation in (1, 4, 16))

    def dilated_spec(dilation):
        return pl.BlockSpec((1, dilation, tm // dilation, QKV_WIDTH),
                            lambda i: (i // pos_blocks, 0, i % pos_blocks, 0))

    return pl.pallas_call(
        functools.partial(_inproj_kernel, sub_rows=sub_rows),
        grid=(t // tm,),
        in_specs=[
            pl.BlockSpec((tm, D_MODEL), lambda i: (i, 0)),
            resident((1, D_MODEL)),
            resident((D_MODEL, SRC_GLR_COL)),
            resident(w_gates.shape),
            resident(w_glr.shape),
            table_spec, table_spec, table_spec,
        ],
        out_specs=[
            pl.BlockSpec((tm, MAIN_WIDTH), lambda i: (i, 0)),
            pl.BlockSpec((tm, LANES), lambda i: (i, 0)),
            dilated_spec(4),
            dilated_spec(16),
        ],
        out_shape=[
            jax.ShapeDtypeStruct((t, MAIN_WIDTH), BF16),
            jax.ShapeDtypeStruct((t, LANES), BF16),
            jax.ShapeDtypeStruct((batch, 4, seq // 4, QKV_WIDTH), BF16),
            jax.ShapeDtypeStruct((batch, 16, seq // 16, QKV_WIDTH), BF16),
        ],
        scratch_shapes=[pltpu.VMEM((3, tm, D_MODEL), BF16)],
        compiler_params=_params(("parallel",)),
        name="inproj",
    )(x2d, g1, w_main, w_gates, w_glr, tab1, tab4, tab16)


def _attn_kernel(q_ref, kp_ref, kc_ref, vp_ref, vc_ref, o_ref, st_ref, *scratch, tq, dilation):
    n = pl.program_id(1)
    blk = KEYS_BACK
    qi = lax.broadcasted_iota(jnp.int32, (blk, 2 * blk), 0)
    jj = lax.broadcasted_iota(jnp.int32, (blk, 2 * blk), 1)
    band = jnp.logical_and(jj >= qi, jj <= qi + blk)
    first_band = jnp.logical_and(band, jj + n * tq >= blk)
    lane = lax.broadcasted_iota(jnp.int32, (blk, LANES), 1)
    neg_inf = jnp.float32(-jnp.inf)

    def windows(c, qb, cur_ref, prev_ref, hs):
        if qb == 0:
            return jnp.concatenate([prev_ref[0, c, :, hs], cur_ref[0, c, 0:blk, hs]], axis=0)
        return cur_ref[0, c, (qb - 1) * blk:(qb + 1) * blk, hs]

    def scores(c, qb):
        valid = first_band if qb == 0 else band
        out = []
        for h in range(HEADS_PER_GROUP):
            hs = slice(h * HEAD_DIM, (h + 1) * HEAD_DIM)
            q = q_ref[0, c, qb * blk:(qb + 1) * blk, hs]
            kk = windows(c, qb, kc_ref, kp_ref, hs)
            s = lax.dot_general(q, kk, (((1,), (1,)), ((), ())), preferred_element_type=F32)
            out.append(jnp.where(valid, s, neg_inf))
        return out

    def finish(c, qb, s_heads):
        rows = slice(qb * blk, (qb + 1) * blk)
        nat = pl.ds(c + qb * blk * dilation, blk, stride=dilation)
        stats = jnp.zeros((blk, LANES), F32)
        for h, s in enumerate(s_heads):
            hs = slice(h * HEAD_DIM, (h + 1) * HEAD_DIM)
            vv = windows(c, qb, vc_ref, vp_ref, hs)
            m = jnp.max(s, axis=-1, keepdims=True)
            p = jnp.exp(s - m)
            l = jnp.sum(p, axis=-1, keepdims=True)
            o = jnp.dot(p.astype(BF16), vv, preferred_element_type=F32) / l
            if dilation == 1:
                o_ref[0, rows, hs] = o.astype(BF16)
            else:
                scratch[0][h, nat, :] = o
            stats = jnp.where(lane == h, m, stats)
            stats = jnp.where(lane == HEADS_PER_GROUP + h, l, stats)
        if dilation == 1:
            st_ref[0, rows, :] = stats
        else:
            st_ref[0, nat, :] = stats

    def run(blocks):
        pending = None
        for c, qb in blocks:
            s_heads = scores(c, qb)
            if pending is not None:
                finish(*pending)
            pending = (c, qb, s_heads)
        finish(*pending)

    if dilation == 1:
        run([(0, qb) for qb in range(tq // blk)])
    else:
        per_trip = max(1, BLOCK_HEADS_PER_TRIP // (HEADS_PER_GROUP * (tq // blk)))

        def body(t, carry):
            run([(t * per_trip + u, qb) for u in range(per_trip) for qb in range(tq // blk)])
            return carry
        lax.fori_loop(0, dilation // per_trip, body, 0)
        for h in range(HEADS_PER_GROUP):
            o_ref[0, :, h * HEAD_DIM:(h + 1) * HEAD_DIM] = scratch[0][h].astype(BF16)


def _attention_group(qkv, col_tiles, dilation, batch, seq, tile_positions):
    r = dilation
    tq = tile_positions // r
    nblk = seq // tile_positions
    prev_per_blk = tq // KEYS_BACK
    cq, ck, cv = col_tiles

    def cur(ct):
        return pl.BlockSpec((1, r, tq, GROUP_WIDTH), lambda b, n: (b, 0, n, ct))

    def prev(ct):
        return pl.BlockSpec((1, r, KEYS_BACK, GROUP_WIDTH),
                            lambda b, n: (b, 0, jnp.maximum(n * prev_per_blk - 1, 0), ct))

    scratch = []
    if r > 1:
        scratch.append(pltpu.VMEM((HEADS_PER_GROUP, tile_positions, HEAD_DIM), F32))
    o, st = pl.pallas_call(
        functools.partial(_attn_kernel, tq=tq, dilation=r),
        grid=(batch, nblk),
        in_specs=[cur(cq), prev(ck), cur(ck), prev(cv), cur(cv)],
        out_specs=[
            pl.BlockSpec((1, tile_positions, GROUP_WIDTH), lambda b, n: (b, n, 0)),
            pl.BlockSpec((1, tile_positions, LANES), lambda b, n: (b, n, 0)),
        ],
        out_shape=[
            jax.ShapeDtypeStruct((batch, seq, GROUP_WIDTH), BF16),
            jax.ShapeDtypeStruct((batch, seq, LANES), F32),
        ],
        scratch_shapes=scratch,
        compiler_params=_params(("parallel", "arbitrary")),
        name=f"attn_r{r}",
    )(qkv, qkv, qkv, qkv, qkv)
    return o.reshape(batch * seq, GROUP_WIDTH), st.reshape(batch * seq, LANES)


def _gla_kernel(q_ref, k_ref, v_ref, gr_ref, glr_ref, wg_ref, bg_ref, gn_ref, tri_ref,
                o_ref, state_sc, *, blk_rows, chunk):
    n = pl.program_id(1)

    @pl.when(n == 0)
    def _():
        state_sc[...] = jnp.zeros_like(state_sc)

    sub = GLA_SUB
    nsub = chunk // sub
    sub_shift = sub.bit_length() - 1
    row = lax.broadcasted_iota(jnp.int32, (chunk, chunk), 0)
    colm = lax.broadcasted_iota(jnp.int32, (chunk, chunk), 1)
    diag_mask = jnp.logical_and((row >> sub_shift) == (colm >> sub_shift), colm <= row)
    dn_t = (((1,), (1,)), ((), ()))
    dn_l = (((0,), (0,)), ((), ()))
    subs = [slice(sb * sub, (sb + 1) * sub) for sb in range(nsub)]
    rows_cat = functools.partial(jnp.concatenate, axis=0)

    def chunk_head(r0, bcum, c, h):
        rows = pl.ds(pl.multiple_of(r0 + c * chunk, chunk), chunk)
        ks = slice(h * GLA_KEY_DIM, (h + 1) * GLA_KEY_DIM)
        vs = slice(h * GLA_VAL_DIM, (h + 1) * GLA_VAL_DIM)
        b = bcum[c * chunk:(c + 1) * chunk, ks]
        q = q_ref[rows, ks].astype(F32)
        k = k_ref[rows, ks].astype(F32)
        v = v_ref[rows, vs]
        cen = [b[sb * sub + sub // 2:sb * sub + sub // 2 + 1, :] for sb in range(nsub)]
        bnd = [None] + [b[sb * sub - 1:sb * sub, :] for sb in range(1, nsub)]
        b_last = b[chunk - 1:chunk, :]
        dev = [b[s, :] - cen[sb] for sb, s in enumerate(subs)]
        q_c = [q[s, :] * jnp.exp(dev[sb]) for sb, s in enumerate(subs)]
        k_c = [k[s, :] * jnp.exp(-dev[sb]) for sb, s in enumerate(subs)]
        a = jnp.where(diag_mask,
                      lax.dot_general(rows_cat(q_c).astype(BF16), rows_cat(k_c).astype(BF16), dn_t,
                                      preferred_element_type=F32), 0.0)
        for sb in range(1, nsub):
            qb = (q_c[sb] * jnp.exp(cen[sb] - bnd[sb])).astype(BF16)
            kb = [k_c[t] * jnp.exp(bnd[sb] - cen[t]) for t in range(sb)]
            kb.append(jnp.zeros((chunk - sb * sub, GLA_KEY_DIM), F32))
            off = lax.dot_general(qb, rows_cat(kb).astype(BF16), dn_t, preferred_element_type=F32)
            pieces = [jnp.zeros((sb * sub, chunk), F32), off]
            if chunk - (sb + 1) * sub:
                pieces.append(jnp.zeros((chunk - (sb + 1) * sub, chunk), F32))
            a = a + rows_cat(pieces)
        q_in = rows_cat([q_c[sb] * jnp.exp(cen[sb]) for sb in range(nsub)]).astype(BF16)
        k_st = rows_cat([k_c[sb] * jnp.exp(b_last - cen[sb]) for sb in range(nsub)]).astype(BF16)
        st = state_sc[h]
        o_inter = lax.dot_general(q_in, st.astype(BF16), dn_t, preferred_element_type=F32)
        state_sc[h] = jnp.exp(b_last) * st + lax.dot_general(v, k_st, dn_l,
                                                               preferred_element_type=F32)
        o = o_inter + jnp.dot(a.astype(BF16), v, preferred_element_type=F32)
        ms = jnp.mean(o * o, axis=-1, keepdims=True)
        y = o * lax.rsqrt(ms + NORM_EPS) * gn_ref[...]
        g = gr_ref[rows, vs].astype(F32)
        o_ref[rows, vs] = (y * (g * jax.nn.sigmoid(g))).astype(BF16)

    def block(i, carry):
        r0 = pl.multiple_of(i * blk_rows, blk_rows)
        z = jnp.dot(glr_ref[pl.ds(r0, blk_rows), :], wg_ref[...],
                    preferred_element_type=F32) + bg_ref[...]
        log_a = (jnp.minimum(z, 0.0) - jnp.log(1.0 + jnp.exp(-jnp.abs(z)))) / GLA_GATE_NORMALIZER
        hi = log_a.astype(BF16)
        lo = (log_a - hi.astype(F32)).astype(BF16)
        bcum = (jnp.dot(tri_ref[...], hi, preferred_element_type=F32)
                + jnp.dot(tri_ref[...], lo, preferred_element_type=F32))
        for c in range(blk_rows // chunk):
            for h in range(GLA_HEADS):
                chunk_head(r0, bcum, c, h)
        return carry

    lax.fori_loop(0, q_ref.shape[0] // blk_rows, block, 0, unroll=GLA_BLOCKS_PER_TRIP)


def _gla(proj, glr, wg, bg, gn, batch, seq, tc, blk_rows, chunk):
    tri = _block_tril(blk_rows, chunk)
    nblk = seq // tc
    row = lambda b, n: b * nblk + n
    return pl.pallas_call(
        functools.partial(_gla_kernel, blk_rows=blk_rows, chunk=chunk),
        grid=(batch, nblk),
        in_specs=[
            pl.BlockSpec((tc, COL_TILE), lambda b, n: (row(b, n), CT_GQ)),
            pl.BlockSpec((tc, COL_TILE), lambda b, n: (row(b, n), CT_GK)),
            pl.BlockSpec((tc, 2 * COL_TILE), lambda b, n: (row(b, n), CT_GV // 2)),
            pl.BlockSpec((tc, 2 * COL_TILE), lambda b, n: (row(b, n), CT_GR // 2)),
            pl.BlockSpec((tc, LANES), lambda b, n: (row(b, n), 0)),
            pl.BlockSpec((LANES, GLA_HEADS * GLA_KEY_DIM), lambda b, n: (0, 0)),
            pl.BlockSpec((1, GLA_HEADS * GLA_KEY_DIM), lambda b, n: (0, 0)),
            pl.BlockSpec((1, GLA_VAL_DIM), lambda b, n: (0, 0)),
            pl.BlockSpec((blk_rows, blk_rows), lambda b, n: (0, 0)),
        ],
        out_specs=pl.BlockSpec((tc, GLA_HEADS * GLA_VAL_DIM), lambda b, n: (row(b, n), 0)),
        out_shape=jax.ShapeDtypeStruct((batch * seq, GLA_HEADS * GLA_VAL_DIM), BF16),
        scratch_shapes=[pltpu.VMEM((GLA_HEADS, GLA_VAL_DIM, GLA_KEY_DIM), F32)],
        compiler_params=_params(("parallel", "arbitrary")),
        name="gla",
    )(proj, proj, proj, proj, glr, wg, bg, gn, tri)


def _rms(x, g):
    ms = jnp.mean(x * x, axis=-1, keepdims=True)
    return x * lax.rsqrt(ms + NORM_EPS) * g


def _post_kernel(x_ref, o1_ref, o2_ref, o3_ref, s1_ref, s2_ref, s3_ref, gla_ref, ga_ref, gb_ref,
                 wa_ref, wb_ref, wo_ref, g2_ref, wi_ref, wd_ref, gf_ref, out_ref, *, sub_rows,
                 ffn_bounds):
    o_refs = (o1_ref, o2_ref, o3_ref)
    s_refs = (s1_ref, s2_ref, s3_ref)
    subs = [slice(r0, r0 + sub_rows) for r0 in range(0, x_ref.shape[0], sub_rows)]
    chunks = list(zip(ffn_bounds[:-1], ffn_bounds[1:]))
    dot = functools.partial(jnp.dot, preferred_element_type=F32)

    def gla_branch(rows):
        return dot(gla_ref[rows, :], wb_ref[...])

    def attn_branch(rows):
        stats = [s[rows, :] for s in s_refs]
        heads = []
        for h in range(HEADS_PER_GROUP):
            hs = slice(h * HEAD_DIM, (h + 1) * HEAD_DIM)
            ms = [s[:, h:h + 1] for s in stats]
            ls = [s[:, HEADS_PER_GROUP + h:HEADS_PER_GROUP + h + 1] for s in stats]
            m_all = jnp.maximum(jnp.maximum(ms[0], ms[1]), ms[2])
            ws = [l * jnp.exp(m - m_all) for m, l in zip(ms, ls)]
            inv = 1.0 / (ws[0] + ws[1] + ws[2])
            acc = (ws[0] * inv) * o_refs[0][rows, hs].astype(F32)
            acc = acc + (ws[1] * inv) * o_refs[1][rows, hs].astype(F32)
            acc = acc + (ws[2] * inv) * o_refs[2][rows, hs].astype(F32)
            heads.append(acc.astype(BF16))
        return dot(jnp.concatenate(heads, axis=1), wa_ref[...])

    def mixer(rows, ya, yb):
        mix = (jax.nn.sigmoid(ga_ref[rows, :].astype(F32)) * ya
               + jax.nn.sigmoid(gb_ref[rows, :].astype(F32)) * yb)
        return x_ref[rows, :] + dot(mix.astype(BF16), wo_ref[...])

    def ffn_up(h2, lo, hi):
        g = dot(h2, wi_ref[:, lo:hi])
        u = dot(h2, wi_ref[:, FFN_HIDDEN + lo:FFN_HIDDEN + hi])
        return (g * jax.nn.sigmoid(g) * u).astype(BF16)

    yb = [gla_branch(rows) for rows in subs]
    ya = [attn_branch(rows) for rows in subs]
    x1 = [mixer(rows, a, b) for rows, a, b in zip(subs, ya, yb)]
    h2 = [_rms(v, g2_ref[...]).astype(BF16) for v in x1]
    acc = x1
    for lo, hi in chunks:
        act = [ffn_up(h, lo, hi) for h in h2]
        acc = [v + dot(a, wd_ref[lo:hi, :]) for v, a in zip(acc, act)]
    for rows, v in zip(subs, acc):
        out_ref[rows, :] = _rms(v, gf_ref[...])


def _post(x2d, attn_o, attn_st, gla_out, proj, wa, wb, wo, g2, wi, wd, gf, tm, sub_rows, ffn_bounds):
    t = x2d.shape[0]
    resident = lambda shape: pl.BlockSpec(shape, lambda i: (0, 0), pipeline_mode=pl.Buffered(1))
    rows = lambda width, col=0: pl.BlockSpec((tm, width), lambda i: (i, col))
    return pl.pallas_call(
        functools.partial(_post_kernel, sub_rows=sub_rows, ffn_bounds=ffn_bounds),
        grid=(t // tm,),
        in_specs=[
            rows(D_MODEL),
            rows(GROUP_WIDTH), rows(GROUP_WIDTH), rows(GROUP_WIDTH),
            rows(LANES), rows(LANES), rows(LANES),
            rows(D_MODEL),
            rows(D_MODEL, CT_GA // 2), rows(D_MODEL, CT_GB // 2),
            resident(wa.shape), resident(wb.shape), resident(wo.shape), resident(g2.shape),
            resident(wi.shape), resident(wd.shape), resident(gf.shape),
        ],
        out_specs=rows(D_MODEL),
        out_shape=jax.ShapeDtypeStruct((t, D_MODEL), F32),
        compiler_params=_params(("parallel",)),
        name="post",
    )(x2d, *attn_o, *attn_st, gla_out, proj, proj, wa, wb, wo, g2, wi, wd, gf)


def _block_tril(n, blk):
    r = np.arange(n)
    return jnp.asarray((r[:, None] // blk == r[None, :] // blk) & (r[None, :] <= r[:, None]), BF16)


def _layer(x2d, batch, seq, norm1_g, w_in, w_gate_lr, b_gate, gla_norm_g, w_branch_a, w_branch_b,
           w_out, norm2_g, w_ffn_in, w_ffn_down, out_g):
    assert w_in.shape[1] == SRC_GLR_COL + GLA_GATE_RANK + 2 * D_MODEL
    w_main = w_in.astype(BF16)
    w_glr = jnp.pad(w_main[:, SRC_GLR_COL:SRC_GLR_COL + GLA_GATE_RANK],
                    ((0, 0), (0, LANES - GLA_GATE_RANK)))
    w_gates = w_main[:, SRC_GLR_COL + GLA_GATE_RANK:]
    wg = jnp.pad(w_gate_lr, ((0, LANES - GLA_GATE_RANK), (0, 0))).astype(BF16)

    proj, glr, qkv2, qkv3 = _inproj(x2d, norm1_g[None, :], w_main, w_gates, w_glr,
                                    batch=batch, seq=seq, tm=ROW_TILE, sub_rows=SUB_ROWS)

    qkv1 = proj.reshape(batch, 1, seq, MAIN_WIDTH)
    attn = [
        _attention_group(qkv1, (CT_Q1, CT_K1, CT_V1), 1, batch, seq, ATTN_TILE_POSITIONS),
        _attention_group(qkv2, (0, 1, 2), 4, batch, seq, ATTN_TILE_POSITIONS),
        _attention_group(qkv3, (0, 1, 2), 16, batch, seq, ATTN_TILE_POSITIONS),
    ]
    attn_o = [o for o, _ in attn]
    attn_st = [st for _, st in attn]

    gla_out = _gla(proj, glr, wg, b_gate[None, :], gla_norm_g[None, :],
                   batch, seq, tc=GLA_STEP_ROWS, blk_rows=GLA_BLOCK_ROWS, chunk=GLA_CHUNK)

    return _post(x2d, attn_o, attn_st, gla_out, proj,
                 w_branch_a.astype(BF16), w_branch_b.astype(BF16), w_out.astype(BF16),
                 norm2_g[None, :], w_ffn_in.astype(BF16), w_ffn_down.astype(BF16), out_g[None, :],
                 tm=ROW_TILE, sub_rows=SUB_ROWS, ffn_bounds=(0, 1536, FFN_HIDDEN))


def kernel(x, norm1_g, w_in, w_gate_lr, b_gate, gla_norm_g, w_branch_a, w_branch_b, w_out, norm2_g,
           w_ffn_in, w_ffn_down, norm_f_g):
    batch, seq, d = x.shape
    depth = w_in.shape[0]
    assert depth == 1 and d == D_MODEL
    x2d = x.reshape(batch * seq, d)
    out = _layer(x2d, batch, seq, norm1_g[0], w_in[0], w_gate_lr[0], b_gate[0], gla_norm_g[0],
                 w_branch_a[0], w_branch_b[0], w_out[0], norm2_g[0], w_ffn_in[0], w_ffn_down[0],
                 norm_f_g)
    return out.reshape(batch, seq, d)
```
